```python
import jax, jax.numpy as jnp
from jax import lax
import numpy as np

D_MODEL = 2048
BATCH = 16
SEQ = 2048
DEPTH = 1

CHUNK = 64
LEFT_CHUNKS = 8
BAND = (LEFT_CHUNKS + 1) * CHUNK

D_MIX = D_MODEL
D_ATTN = D_MIX // 2
ATTN_HEADS = 16
ATTN_HEAD_DIM = D_ATTN // ATTN_HEADS
REL_CLIP = 128
D_POOL = D_MIX - D_ATTN
POOL_WINDOWS = (2, 4, 8, 16)
N_POOL_GROUPS = len(POOL_WINDOWS)
POOL_GROUP_DIM = D_POOL // N_POOL_GROUPS
D_IN = 3 * D_ATTN + D_POOL

N_MEM = 256
CROSS_HEADS = 4
CROSS_HEAD_DIM = 128
D_CROSS = CROSS_HEADS * CROSS_HEAD_DIM

D_FF = ((8 * D_MODEL // 3) + 255) // 256 * 256
FFN_RES_WEIGHT = 0.5
EPS = 1e-6
NEG_INF = -1e30

kernel_name = "hybrid_chunk_attn_pool_macaron"


def rmsnorm(x, g):
    xf = x.astype(jnp.float32)
    y = xf * lax.rsqrt(jnp.mean(xf * xf, axis=-1, keepdims=True) + EPS)
    return (y * g.astype(jnp.float32)).astype(x.dtype)


def swiglu(x, w_gate, w_up, w_down):
    return (jax.nn.silu(x @ w_gate) * (x @ w_up)) @ w_down


def chunk_rel_attention(q, k, v, rel_table):
    B, S, H, Dh = q.shape
    nc = S // CHUNK
    pad = LEFT_CHUNKS * CHUNK
    kp = jnp.pad(k, ((0, 0), (pad, 0), (0, 0), (0, 0)))
    vp = jnp.pad(v, ((0, 0), (pad, 0), (0, 0), (0, 0)))
    q_band = pad + jnp.arange(CHUNK)
    rel = q_band[:, None] - jnp.arange(BAND)[None, :]
    rel_idx = jnp.clip(rel, -REL_CLIP, REL_CLIP) + REL_CLIP
    bias = rel_table[:, rel_idx].astype(jnp.float32)
    scale = Dh ** -0.5
    qc = q.reshape(B, nc, CHUNK, H, Dh).transpose(1, 0, 2, 3, 4)

    def one_chunk(args):
        c, qb = args
        start = c * CHUNK
        kb = lax.dynamic_slice_in_dim(kp, start, BAND, axis=1)
        vb = lax.dynamic_slice_in_dim(vp, start, BAND, axis=1)
        s = jnp.einsum('bqhd,bkhd->bhqk', qb, kb).astype(jnp.float32) * scale + bias[None]
        valid = (start + jnp.arange(BAND)) >= pad
        s = jnp.where(valid[None, None, None, :], s, NEG_INF)
        p = jax.nn.softmax(s, axis=-1).astype(vb.dtype)
        return jnp.einsum('bhqk,bkhd->bqhd', p, vb)

    out = lax.map(one_chunk, (jnp.arange(nc), qc))
    return out.transpose(1, 0, 2, 3, 4).reshape(B, S, H * Dh)


def multiscale_pool(u, w_pool, pool_scale):
    B, S, _ = u.shape
    uf = u.astype(jnp.float32)
    cs = jnp.concatenate([jnp.zeros((B, 1, D_POOL), jnp.float32), jnp.cumsum(uf, axis=1)], axis=1)
    t = jnp.arange(S)
    diffs = []
    for g, w in enumerate(POOL_WINDOWS):
        lo = jnp.maximum(t + 1 - w, 0)
        count = (t + 1 - lo).astype(jnp.float32)
        csg = cs[..., g * POOL_GROUP_DIM:(g + 1) * POOL_GROUP_DIM]
        mean = (csg[:, 1:] - csg[:, lo]) / count[None, :, None]
        diffs.append(mean - uf[..., g * POOL_GROUP_DIM:(g + 1) * POOL_GROUP_DIM])
    d = jnp.stack(diffs, axis=2).astype(u.dtype)
    y = jnp.einsum('bsgc,gcd->bsgd', d, w_pool).reshape(B, S, D_POOL)
    return y * pool_scale


def memory_cross_attention(h, mem, w_cq, w_ckv, w_co):
    B, S, _ = h.shape
    q = (h @ w_cq).reshape(B, S, CROSS_HEADS, CROSS_HEAD_DIM)
    kv = mem @ w_ckv
    k = kv[..., :D_CROSS].reshape(B, N_MEM, CROSS_HEADS, CROSS_HEAD_DIM)
    v = kv[..., D_CROSS:].reshape(B, N_MEM, CROSS_HEADS, CROSS_HEAD_DIM)
    s = jnp.einsum('bshd,bmhd->bhsm', q, k).astype(jnp.float32) * (CROSS_HEAD_DIM ** -0.5)
    p = jax.nn.softmax(s, axis=-1).astype(v.dtype)
    o = jnp.einsum('bhsm,bmhd->bshd', p, v).reshape(B, S, D_CROSS)
    return o @ w_co


def setup_inputs(seed: int = 0) -> dict:
    key = jax.random.key(seed)
    ks = jax.random.split(key, 24)
    f32 = jnp.float32

    def w(k, shape, fan_in):
        return jax.random.normal(k, shape, f32) * (fan_in ** -0.5)

    def gain(k, shape):
        return 1.0 + 0.05 * jax.random.normal(k, shape, f32)

    L = DEPTH
    return {
        "x": jax.random.normal(ks[0], (BATCH, SEQ, D_MODEL), f32),
        "mem": jax.random.normal(ks[1], (BATCH, N_MEM, D_MODEL), f32),
        "ffn1_norm": gain(ks[2], (L, D_MODEL)),
        "ffn1_w_gate": w(ks[3], (L, D_MODEL, D_FF), D_MODEL),
        "ffn1_w_up": w(ks[4], (L, D_MODEL, D_FF), D_MODEL),
        "ffn1_w_down": w(ks[5], (L, D_FF, D_MODEL), D_FF),
        "mix_norm": gain(ks[6], (L, D_MODEL)),
        "w_in": w(ks[7], (L, D_MODEL, D_IN), D_MODEL),
        "rel_bias": 0.5 * jax.random.normal(ks[8], (L, ATTN_HEADS, 2 * REL_CLIP + 1), f32),
        "w_pool": w(ks[9], (L, N_POOL_GROUPS, POOL_GROUP_DIM, POOL_GROUP_DIM), POOL_GROUP_DIM),
        "pool_scale": gain(ks[10], (L, D_POOL)),
        "w_out": w(ks[11], (L, D_MIX, D_MODEL), D_MIX),
        "cross_norm": gain(ks[12], (L, D_MODEL)),
        "mem_norm": gain(ks[13], (L, D_MODEL)),
        "w_cq": w(ks[14], (L, D_MODEL, D_CROSS), D_MODEL),
        "w_ckv": w(ks[15], (L, D_MODEL, 2 * D_CROSS), D_MODEL),
        "w_co": w(ks[16], (L, D_CROSS, D_MODEL), D_CROSS),
        "ffn2_norm": gain(ks[17], (L, D_MODEL)),
        "ffn2_w_gate": w(ks[18], (L, D_MODEL, D_FF), D_MODEL),
        "ffn2_w_up": w(ks[19], (L, D_MODEL, D_FF), D_MODEL),
        "ffn2_w_down": w(ks[20], (L, D_FF, D_MODEL), D_FF),
        "final_norm": gain(ks[21], (D_MODEL,)),
    }


def reference(x, mem, ffn1_norm, ffn1_w_gate, ffn1_w_up, ffn1_w_down, mix_norm, w_in,
              rel_bias, w_pool, pool_scale, w_out, cross_norm, mem_norm, w_cq, w_ckv, w_co,
              ffn2_norm, ffn2_w_gate, ffn2_w_up, ffn2_w_down, final_norm):
    B, S, _ = x.shape
    h = x
    for l in range(DEPTH):
        h = h + FFN_RES_WEIGHT * swiglu(rmsnorm(h, ffn1_norm[l]), ffn1_w_gate[l], ffn1_w_up[l], ffn1_w_down[l])
        z = rmsnorm(h, mix_norm[l]) @ w_in[l]
        q = z[..., 0 * D_ATTN:1 * D_ATTN].reshape(B, S, ATTN_HEADS, ATTN_HEAD_DIM)
        k = z[..., 1 * D_ATTN:2 * D_ATTN].reshape(B, S, ATTN_HEADS, ATTN_HEAD_DIM)
        v = z[..., 2 * D_ATTN:3 * D_ATTN].reshape(B, S, ATTN_HEADS, ATTN_HEAD_DIM)
        u = z[..., 3 * D_ATTN:]
        y_attn = chunk_rel_attention(q, k, v, rel_bias[l])
        y_pool = multiscale_pool(u, w_pool[l], pool_scale[l])
        h = h + jnp.concatenate([y_attn, y_pool], axis=-1) @ w_out[l]
        h = h + memory_cross_attention(rmsnorm(h, cross_norm[l]), rmsnorm(mem, mem_norm[l]),
                                       w_cq[l], w_ckv[l], w_co[l])
        h = h + FFN_RES_WEIGHT * swiglu(rmsnorm(h, ffn2_norm[l]), ffn2_w_gate[l], ffn2_w_up[l], ffn2_w_down[l])
    return rmsnorm(h, final_norm)
```

```python
import functools

import jax
import jax.numpy as jnp
from jax import lax
from jax.experimental import pallas as pl
from jax.experimental.pallas import tpu as pltpu

F32 = jnp.float32
BF16 = jnp.bfloat16

EPS = 1e-6
NEG_INF = -1e30
FFN_RES_WEIGHT = 0.5

CHUNK = 64
LEFT_CHUNKS = 8
REL_CLIP = 128
ATTN_HEADS = 16
ATTN_HEAD_DIM = 64
POOL_WINDOWS = (2, 4, 8, 16)
CROSS_HEADS = 4
CROSS_HEAD_DIM = 128

V7X_LANES = 128
V7X_VMEM_BYTES = 64 * 1024 * 1024

TOKEN_TILE = 512
FF_TILE = 512
Q_BLOCK = 2 * CHUNK
LEFT = LEFT_CHUNKS * CHUNK
KEY_BAND = LEFT + Q_BLOCK
POOL_HALO = 16
VMEM_LIMIT = 56 * 1024 * 1024

_NT = (((1,), (1,)), ((), ()))


def _rms(x, g):
    ms = jnp.mean(x * x, axis=-1, keepdims=True)
    return x * lax.rsqrt(ms + EPS) * g


def _const_spec(shape):
    nd = len(shape)
    return pl.BlockSpec(shape, lambda *_: (0,) * nd, pipeline_mode=pl.Buffered(1))


def _ffn_body(x_ref, g_ref, wg_ref, wu_ref, wd_ref, fin_ref, o_ref, xn_ref, *, final_norm):
    k = pl.program_id(1)

    @pl.when(k == 0)
    def _():
        x = x_ref[...]
        xn_ref[...] = _rms(x, g_ref[...]).astype(BF16)
        o_ref[...] = x

    xn = xn_ref[...]
    g = jnp.dot(xn, wg_ref[...], preferred_element_type=F32)
    u = jnp.dot(xn, wu_ref[...], preferred_element_type=F32)
    a = (g * jax.nn.sigmoid(g)) * u
    o_ref[...] += jnp.dot(a.astype(BF16), wd_ref[...], preferred_element_type=F32)

    if final_norm:
        @pl.when(k == pl.num_programs(1) - 1)
        def _():
            o_ref[...] = _rms(o_ref[...], fin_ref[...])


def _ffn(h, gain, wg, wu, wd_half, fin, *, final_norm):
    t, d = h.shape
    dff = wg.shape[1]
    grid = (t // TOKEN_TILE, dff // FF_TILE)
    return pl.pallas_call(
        functools.partial(_ffn_body, final_norm=final_norm),
        grid=grid,
        in_specs=[
            pl.BlockSpec((TOKEN_TILE, d), lambda i, k: (i, 0)),
            pl.BlockSpec((1, d), lambda i, k: (0, 0)),
            pl.BlockSpec((d, FF_TILE), lambda i, k: (0, k)),
            pl.BlockSpec((d, FF_TILE), lambda i, k: (0, k)),
            pl.BlockSpec((FF_TILE, d), lambda i, k: (k, 0)),
            pl.BlockSpec((1, d), lambda i, k: (0, 0)),
        ],
        out_specs=pl.BlockSpec((TOKEN_TILE, d), lambda i, k: (i, 0)),
        out_shape=jax.ShapeDtypeStruct((t, d), F32),
        scratch_shapes=[pltpu.VMEM((TOKEN_TILE, d), BF16)],
        compiler_params=pltpu.CompilerParams(
            dimension_semantics=("parallel", "arbitrary"), vmem_limit_bytes=VMEM_LIMIT),
        name="ffn_final" if final_norm else "ffn",
    )(h, gain, wg, wu, wd_half, fin)


def _proj_body(h_ref, g_ref, w_ref, z_ref, *, n_split):
    xn = _rms(h_ref[...], g_ref[...]).astype(BF16)
    n = w_ref.shape[1] // n_split
    for c in range(n_split):
        z = jnp.dot(xn, w_ref[:, c * n:(c + 1) * n], preferred_element_type=F32)
        z_ref[:, c * n:(c + 1) * n] = z.astype(z_ref.dtype)


def _proj(h, gain, w, *, n_split, name):
    t, d = h.shape
    n = w.shape[1]
    return pl.pallas_call(
        functools.partial(_proj_body, n_split=n_split),
        grid=(t // TOKEN_TILE,),
        in_specs=[
            pl.BlockSpec((TOKEN_TILE, d), lambda i: (i, 0)),
            _const_spec((1, d)),
            _const_spec((d, n)),
        ],
        out_specs=pl.BlockSpec((TOKEN_TILE, n), lambda i: (i, 0)),
        out_shape=jax.ShapeDtypeStruct((t, n), BF16),
        compiler_params=pltpu.CompilerParams(
            dimension_semantics=("parallel",), vmem_limit_bytes=VMEM_LIMIT),
        name=name,
    )(h, gain, w)


def _mix_body(h_ref, q_ref, kp_ref, kc_ref, vp_ref, vc_ref, uh_ref, uc_ref, bias_ref,
              wp_ref, ps_ref, wo_ref, o_ref, kk_ref, vv_ref, ue_ref, y_ref):
    j = pl.program_id(1)
    tile = q_ref.shape[0]
    d_attn = q_ref.shape[1]
    n_pairs = d_attn // V7X_LANES

    kk_ref[0:tile, :] = kp_ref[...]
    kk_ref[tile:2 * tile, :] = kc_ref[...]
    vv_ref[0:tile, :] = vp_ref[...]
    vv_ref[tile:2 * tile, :] = vc_ref[...]

    lane = lax.broadcasted_iota(jnp.int32, (Q_BLOCK, V7X_LANES), 1)
    low = lane < ATTN_HEAD_DIM
    kbi = lax.broadcasted_iota(jnp.int32, (1, KEY_BAND), 1)

    def block(i, carry):
        r0 = pl.multiple_of(i * Q_BLOCK, Q_BLOCK)
        b0 = pl.multiple_of(tile - LEFT + r0, Q_BLOCK)
        kpos = j * tile + r0 - LEFT + kbi
        startmask = jnp.where(kpos >= 0, 0.0, NEG_INF).astype(F32)
        for p in range(n_pairs):
            cols = slice(p * V7X_LANES, (p + 1) * V7X_LANES)
            qb = q_ref[pl.ds(r0, Q_BLOCK), cols]
            zero = jnp.zeros_like(qb)
            qbd = jnp.concatenate([jnp.where(low, qb, zero), jnp.where(low, zero, qb)], axis=0)
            kb = kk_ref[pl.ds(b0, KEY_BAND), cols]
            s = lax.dot_general(qbd, kb, _NT, preferred_element_type=F32)
            s = s + bias_ref[p] + startmask
            m = jnp.max(s, axis=-1, keepdims=True)
            e = jnp.exp(s - m)
            l = jnp.sum(e, axis=-1, keepdims=True)
            vb = vv_ref[pl.ds(b0, KEY_BAND), cols]
            o = jnp.dot(e.astype(BF16), vb, preferred_element_type=F32) / l
            op = jnp.where(low, o[0:Q_BLOCK], o[Q_BLOCK:2 * Q_BLOCK])
            y_ref[pl.ds(r0, Q_BLOCK), cols] = op.astype(BF16)
        return carry

    lax.fori_loop(0, tile // Q_BLOCK, block, 0)

    halo = uh_ref[...].astype(F32)
    ue_ref[0:POOL_HALO, :] = jnp.where(j == 0, jnp.zeros_like(halo), halo)
    ue_ref[POOL_HALO:POOL_HALO + tile, :] = uc_ref[...].astype(F32)
    t_glob = lax.broadcasted_iota(jnp.int32, (tile, 1), 0) + j * tile
    gdim = uc_ref.shape[1] // len(POOL_WINDOWS)
    for g, w in enumerate(POOL_WINDOWS):
        cols = slice(g * gdim, (g + 1) * gdim)
        cur = ue_ref[POOL_HALO:POOL_HALO + tile, cols]
        acc = cur
        for sft in range(1, w):
            acc = acc + ue_ref[POOL_HALO - sft:POOL_HALO - sft + tile, cols]
        cnt = jnp.minimum(t_glob + 1, w).astype(F32)
        dlt = (acc / cnt - cur).astype(BF16)
        yp = jnp.dot(dlt, wp_ref[g], preferred_element_type=F32) * ps_ref[:, cols]
        y_ref[:, d_attn + g * gdim:d_attn + (g + 1) * gdim] = yp.astype(BF16)

    o_ref[...] = h_ref[...] + jnp.dot(y_ref[...], wo_ref[...], preferred_element_type=F32)


def _mix(h, z, bias, w_pool, pool_scale, w_out, *, batch, seq):
    t, d = h.shape
    d_attn = ATTN_HEADS * ATTN_HEAD_DIM
    d_pool = z.shape[1] - 3 * d_attn
    assert d_attn == d_pool
    tile = TOKEN_TILE
    tps = seq // tile
    hpt = tile // POOL_HALO

    def cur(col):
        return pl.BlockSpec((tile, d_attn), lambda b, j: (b * tps + j, col))

    def prev(col):
        return pl.BlockSpec((tile, d_attn), lambda b, j: (jnp.maximum(b * tps + j - 1, 0), col))

    halo_cols = d_pool // d_attn * 3
    in_specs = [
        pl.BlockSpec((tile, d), lambda b, j: (b * tps + j, 0)),
        cur(0),
        prev(1), cur(1),
        prev(2), cur(2),
        pl.BlockSpec((POOL_HALO, d_pool),
                     lambda b, j: (jnp.maximum((b * tps + j) * hpt - 1, 0), halo_cols)),
        cur(3),
        _const_spec(bias.shape),
        _const_spec(w_pool.shape),
        _const_spec(pool_scale.shape),
        _const_spec(w_out.shape),
    ]
    return pl.pallas_call(
        _mix_body,
        grid=(batch, tps),
        in_specs=in_specs,
        out_specs=pl.BlockSpec((tile, d), lambda b, j: (b * tps + j, 0)),
        out_shape=jax.ShapeDtypeStruct((t, d), F32),
        scratch_shapes=[
            pltpu.VMEM((2 * tile, d_attn), BF16),
            pltpu.VMEM((2 * tile, d_attn), BF16),
            pltpu.VMEM((POOL_HALO + tile, d_pool), F32),
            pltpu.VMEM((tile, d_attn + d_pool), BF16),
        ],
        compiler_params=pltpu.CompilerParams(
            dimension_semantics=("parallel", "arbitrary"), vmem_limit_bytes=VMEM_LIMIT),
        name="mix",
    )(h, z, z, z, z, z, z, z, bias, w_pool, pool_scale, w_out)


def _attn_bias(rel_table):
    qi = jnp.arange(Q_BLOCK)[:, None]
    kb = jnp.arange(KEY_BAND)[None, :]
    rel = qi + LEFT - kb
    idx = jnp.clip(rel, -REL_CLIP, REL_CLIP) + REL_CLIP
    kc = kb // CHUNK
    qc = LEFT_CHUNKS + qi // CHUNK
    valid = (kc <= qc) & (kc >= qc - LEFT_CHUNKS)
    b = jnp.where(valid[None], rel_table[:, idx].astype(F32), NEG_INF)
    h = rel_table.shape[0]
    return b.reshape(h // 2, 2 * Q_BLOCK, KEY_BAND)


def _cross_body(h_ref, g_ref, wq_ref, k_ref, v_ref, wo_ref, o_ref, oc_ref):
    h = h_ref[...]
    xn = _rms(h, g_ref[...]).astype(BF16)
    q = jnp.dot(xn, wq_ref[...], preferred_element_type=F32).astype(BF16)
    scale = CROSS_HEAD_DIM ** -0.5
    for hh in range(CROSS_HEADS):
        cols = slice(hh * CROSS_HEAD_DIM, (hh + 1) * CROSS_HEAD_DIM)
        s = lax.dot_general(q[:, cols], k_ref[:, cols], _NT, preferred_element_type=F32) * scale
        m = jnp.max(s, axis=-1, keepdims=True)
        e = jnp.exp(s - m)
        l = jnp.sum(e, axis=-1, keepdims=True)
        o = jnp.dot(e.astype(BF16), v_ref[:, cols], preferred_element_type=F32) / l
        oc_ref[:, cols] = o.astype(BF16)
    o_ref[...] = h + jnp.dot(oc_ref[...], wo_ref[...], preferred_element_type=F32)


def _cross(h, gain, w_cq, kv, w_co, *, seq, n_mem):
    t, d = h.shape
    dc = w_cq.shape[1]
    tps = seq // TOKEN_TILE
    return pl.pallas_call(
        _cross_body,
        grid=(t // TOKEN_TILE,),
        in_specs=[
            pl.BlockSpec((TOKEN_TILE, d), lambda i: (i, 0)),
            _const_spec((1, d)),
            _const_spec(w_cq.shape),
            pl.BlockSpec((n_mem, dc), lambda i: (i // tps, 0)),
            pl.BlockSpec((n_mem, dc), lambda i: (i // tps, 1)),
            _const_spec(w_co.shape),
        ],
        out_specs=pl.BlockSpec((TOKEN_TILE, d), lambda i: (i, 0)),
        out_shape=jax.ShapeDtypeStruct((t, d), F32),
        scratch_shapes=[pltpu.VMEM((TOKEN_TILE, dc), BF16)],
        compiler_params=pltpu.CompilerParams(
            dimension_semantics=("parallel",), vmem_limit_bytes=VMEM_LIMIT),
        name="cross",
    )(h, gain, w_cq, kv, kv, w_co)


def kernel(x, mem, ffn1_norm, ffn1_w_gate, ffn1_w_up, ffn1_w_down, mix_norm, w_in, rel_bias,
           w_pool, pool_scale, w_out, cross_norm, mem_norm, w_cq, w_ckv, w_co, ffn2_norm,
           ffn2_w_gate, ffn2_w_up, ffn2_w_down, final_norm):
    batch, seq, d = x.shape
    n_mem = mem.shape[1]
    depth = ffn1_norm.shape[0]
    d_attn = ATTN_HEADS * ATTN_HEAD_DIM
    assert seq % TOKEN_TILE == 0 and TOKEN_TILE % Q_BLOCK == 0 and TOKEN_TILE >= LEFT

    h = x.reshape(batch * seq, d)
    mem2 = mem.reshape(batch * n_mem, d)
    fin = final_norm.reshape(1, d)
    qscale = jnp.where(jnp.arange(w_in.shape[-1]) < d_attn, ATTN_HEAD_DIM ** -0.5, 1.0).astype(F32)

    for l in range(depth):
        last = l == depth - 1
        h = _ffn(h, ffn1_norm[l].reshape(1, d), ffn1_w_gate[l].astype(BF16),
                 ffn1_w_up[l].astype(BF16), (FFN_RES_WEIGHT * ffn1_w_down[l]).astype(BF16), fin,
                 final_norm=False)
        z = _proj(h, mix_norm[l].reshape(1, d), (w_in[l] * qscale).astype(BF16),
                  n_split=4, name="proj")
        h = _mix(h, z, _attn_bias(rel_bias[l]), w_pool[l].astype(BF16),
                 pool_scale[l].reshape(1, -1), w_out[l].astype(BF16), batch=batch, seq=seq)
        kv = _proj(mem2, mem_norm[l].reshape(1, d), w_ckv[l].astype(BF16), n_split=1, name="memkv")
        h = _cross(h, cross_norm[l].reshape(1, d), w_cq[l].astype(BF16), kv,
                   w_co[l].astype(BF16), seq=seq, n_mem=n_mem)
        h = _ffn(h, ffn2_norm[l].reshape(1, d), ffn2_w_gate[l].astype(BF16),
                 ffn2_w_up[l].astype(BF16), (FFN_RES_WEIGHT * ffn2_w_down[l]).astype(BF16), fin,
                 final_norm=last)
    return h.reshape(batch, seq, d)
```

```python
import functools

import jax
import jax.numpy as jnp
from jax import lax
from jax.experimental import pallas as pl
from jax.experimental.pallas import tpu as pltpu

F32 = jnp.float32
BF16 = jnp.bfloat16

EPS = 1e-6
NEG_INF = -1e30
FFN_RES_WEIGHT = 0.5
LOG2_E = 1.4426950408889634

CHUNK = 64
LEFT_CHUNKS = 8
REL_CLIP = 128
ATTN_HEADS = 16
ATTN_HEAD_DIM = 64
POOL_WINDOWS = (2, 4, 8, 16)
CROSS_HEADS = 4
CROSS_HEAD_DIM = 128

V7X_LANES = 128
V7X_VMEM_BYTES = 64 * 1024 * 1024

TOKEN_TILE = 512
FF_TILE = 512
Q_BLOCK = 2 * CHUNK
LEFT = LEFT_CHUNKS * CHUNK
KEY_BAND = LEFT + Q_BLOCK
POOL_HALO = 16
VMEM_LIMIT = 56 * 1024 * 1024
FFN_TOKEN_TILE = 1024
FFN_VMEM_LIMIT = 60 * 1024 * 1024
MIX_VMEM_LIMIT = 61 * 1024 * 1024

_NT = (((1,), (1,)), ((), ()))


def _rms(x, g):
    ms = jnp.mean(x * x, axis=-1, keepdims=True)
    return x * lax.rsqrt(ms + EPS) * g


def _const_spec(shape):
    nd = len(shape)
    return pl.BlockSpec(shape, lambda *_: (0,) * nd, pipeline_mode=pl.Buffered(1))


def _ffn_body(x_ref, g_ref, wg_ref, wu_ref, wd_ref, fin_ref, o_ref, xn_ref, *, final_norm):
    k = pl.program_id(1)

    @pl.when(k == 0)
    def _():
        x = x_ref[...]
        xn_ref[...] = _rms(x, g_ref[...]).astype(BF16)
        o_ref[...] = x

    xn = xn_ref[...]
    g = jnp.dot(xn, wg_ref[...], preferred_element_type=F32)
    u = jnp.dot(xn, wu_ref[...], preferred_element_type=F32)
    a = (g * jax.nn.sigmoid(g)) * u
    o_ref[...] += jnp.dot(a.astype(BF16), wd_ref[...], preferred_element_type=F32)

    if final_norm:
        @pl.when(k == pl.num_programs(1) - 1)
        def _():
            o_ref[...] = _rms(o_ref[...], fin_ref[...])


def _ffn(h, gain, wg, wu, wd_half, fin, *, final_norm):
    t, d = h.shape
    dff = wg.shape[1]
    grid = (t // FFN_TOKEN_TILE, dff // FF_TILE)
    return pl.pallas_call(
        functools.partial(_ffn_body, final_norm=final_norm),
        grid=grid,
        in_specs=[
            pl.BlockSpec((FFN_TOKEN_TILE, d), lambda i, k: (i, 0)),
            pl.BlockSpec((1, d), lambda i, k: (0, 0)),
            pl.BlockSpec((d, FF_TILE), lambda i, k: (0, k)),
            pl.BlockSpec((d, FF_TILE), lambda i, k: (0, k)),
            pl.BlockSpec((FF_TILE, d), lambda i, k: (k, 0)),
            pl.BlockSpec((1, d), lambda i, k: (0, 0)),
        ],
        out_specs=pl.BlockSpec((FFN_TOKEN_TILE, d), lambda i, k: (i, 0)),
        out_shape=jax.ShapeDtypeStruct((t, d), F32),
        scratch_shapes=[pltpu.VMEM((FFN_TOKEN_TILE, d), BF16)],
        compiler_params=pltpu.CompilerParams(
            dimension_semantics=("parallel", "arbitrary"), vmem_limit_bytes=FFN_VMEM_LIMIT),
        name="ffn_final" if final_norm else "ffn",
    )(h, gain, wg, wu, wd_half, fin)


def _proj_body(h_ref, g_ref, w_ref, z_ref, *, n_split):
    xn = _rms(h_ref[...], g_ref[...]).astype(BF16)
    n = w_ref.shape[1] // n_split
    for c in range(n_split):
        z = jnp.dot(xn, w_ref[:, c * n:(c + 1) * n], preferred_element_type=F32)
        z_ref[:, c * n:(c + 1) * n] = z.astype(z_ref.dtype)


def _proj(h, gain, w, *, n_split, name):
    t, d = h.shape
    n = w.shape[1]
    return pl.pallas_call(
        functools.partial(_proj_body, n_split=n_split),
        grid=(t // TOKEN_TILE,),
        in_specs=[
            pl.BlockSpec((TOKEN_TILE, d), lambda i: (i, 0)),
            _const_spec((1, d)),
            _const_spec((d, n)),
        ],
        out_specs=pl.BlockSpec((TOKEN_TILE, n), lambda i: (i, 0)),
        out_shape=jax.ShapeDtypeStruct((t, n), BF16),
        compiler_params=pltpu.CompilerParams(
            dimension_semantics=("parallel",), vmem_limit_bytes=VMEM_LIMIT),
        name=name,
    )(h, gain, w)


def _mix_body(h_ref, q_ref, kp_ref, kc_ref, vp_ref, vc_ref, uh_ref, uc_ref, bias_ref,
              wp_ref, ps_ref, wo_ref, o_ref, kk_ref, vv_ref, mk_ref, ue_ref, y_ref):
    j = pl.program_id(1)
    tile = q_ref.shape[0]
    d_attn = q_ref.shape[1]
    n_pairs = d_attn // V7X_LANES

    kk_ref[0:tile, :] = kp_ref[...]
    kk_ref[tile:2 * tile, :] = kc_ref[...]
    vv_ref[0:tile, :] = vp_ref[...]
    vv_ref[tile:2 * tile, :] = vc_ref[...]

    mlane = lax.broadcasted_iota(jnp.int32, (tile, V7X_LANES), 1)
    mk_ref[0:tile, :] = jnp.where((mlane == 0) & (j == 0), NEG_INF, 0.0).astype(BF16)
    mk_ref[tile:2 * tile, :] = jnp.zeros((tile, V7X_LANES), BF16)

    lane = lax.broadcasted_iota(jnp.int32, (Q_BLOCK, V7X_LANES), 1)
    low = lane < ATTN_HEAD_DIM
    lane2 = lax.broadcasted_iota(jnp.int32, (2 * Q_BLOCK, V7X_LANES), 1)
    one_lane = jnp.where(lane2 == 0, 1.0, 0.0).astype(BF16)
    ones_cols = jnp.ones((KEY_BAND, V7X_LANES), BF16)

    def block(i, carry):
        r0 = pl.multiple_of(i * Q_BLOCK, Q_BLOCK)
        b0 = pl.multiple_of(tile - LEFT + r0, Q_BLOCK)
        mkb = mk_ref[pl.ds(b0, KEY_BAND), :]

        def scores(p):
            cols = slice(p * V7X_LANES, (p + 1) * V7X_LANES)
            qb = q_ref[pl.ds(r0, Q_BLOCK), cols]
            zero = jnp.zeros_like(qb)
            qbd = jnp.concatenate([jnp.where(low, qb, zero), jnp.where(low, zero, qb)], axis=0)
            lhs = jnp.concatenate([qbd, one_lane], axis=1)
            rhs = jnp.concatenate([kk_ref[pl.ds(b0, KEY_BAND), cols], mkb], axis=1)
            return lax.dot_general(lhs, rhs, _NT, preferred_element_type=F32)

        s_all = [scores(p) + bias_ref[p] for p in range(n_pairs)]
        m_all = [jnp.max(s, axis=-1, keepdims=True) for s in s_all]
        e_all = [jnp.exp2(s - m).astype(BF16) for s, m in zip(s_all, m_all)]
        for p in range(n_pairs):
            cols = slice(p * V7X_LANES, (p + 1) * V7X_LANES)
            rhs = jnp.concatenate([vv_ref[pl.ds(b0, KEY_BAND), cols], ones_cols], axis=1)
            ol = jnp.dot(e_all[p], rhs, preferred_element_type=F32)
            o = ol[:, 0:V7X_LANES] / ol[:, V7X_LANES:2 * V7X_LANES]
            op = jnp.where(low, o[0:Q_BLOCK], o[Q_BLOCK:2 * Q_BLOCK])
            y_ref[pl.ds(r0, Q_BLOCK), cols] = op.astype(BF16)
        return carry

    lax.fori_loop(0, tile // Q_BLOCK, block, 0)

    halo = uh_ref[...].astype(F32)
    ue_ref[0:POOL_HALO, :] = jnp.where(j == 0, jnp.zeros_like(halo), halo)
    ue_ref[POOL_HALO:POOL_HALO + tile, :] = uc_ref[...].astype(F32)
    t_glob = lax.broadcasted_iota(jnp.int32, (tile, 1), 0) + j * tile
    gdim = uc_ref.shape[1] // len(POOL_WINDOWS)
    for g, w in enumerate(POOL_WINDOWS):
        cols = slice(g * gdim, (g + 1) * gdim)
        cur = ue_ref[POOL_HALO:POOL_HALO + tile, cols]
        acc = cur
        for sft in range(1, w):
            acc = acc + ue_ref[POOL_HALO - sft:POOL_HALO - sft + tile, cols]
        cnt = jnp.minimum(t_glob + 1, w).astype(F32)
        dlt = (acc / cnt - cur).astype(BF16)
        yp = jnp.dot(dlt, wp_ref[g], preferred_element_type=F32) * ps_ref[:, cols]
        y_ref[:, d_attn + g * gdim:d_attn + (g + 1) * gdim] = yp.astype(BF16)

    o_ref[...] = h_ref[...] + jnp.dot(y_ref[...], wo_ref[...], preferred_element_type=F32)


def _mix(h, z, bias, w_pool, pool_scale, w_out, *, batch, seq):
    t, d = h.shape
    d_attn = ATTN_HEADS * ATTN_HEAD_DIM
    d_pool = z.shape[1] - 3 * d_attn
    assert d_attn == d_pool
    tile = TOKEN_TILE
    tps = seq // tile
    hpt = tile // POOL_HALO

    def cur(col):
        return pl.BlockSpec((tile, d_attn), lambda b, j: (b * tps + j, col))

    def prev(col):
        return pl.BlockSpec((tile, d_attn), lambda b, j: (jnp.maximum(b * tps + j - 1, 0), col))

    halo_cols = d_pool // d_attn * 3
    in_specs = [
        pl.BlockSpec((tile, d), lambda b, j: (b * tps + j, 0)),
        cur(0),
        prev(1), cur(1),
        prev(2), cur(2),
        pl.BlockSpec((POOL_HALO, d_pool),
                     lambda b, j: (jnp.maximum((b * tps + j) * hpt - 1, 0), halo_cols)),
        cur(3),
        _const_spec(bias.shape),
        _const_spec(w_pool.shape),
        _const_spec(pool_scale.shape),
        _const_spec(w_out.shape),
    ]
    return pl.pallas_call(
        _mix_body,
        grid=(batch, tps),
        in_specs=in_specs,
        out_specs=pl.BlockSpec((tile, d), lambda b, j: (b * tps + j, 0)),
        out_shape=jax.ShapeDtypeStruct((t, d), F32),
        scratch_shapes=[
            pltpu.VMEM((2 * tile, d_attn), BF16),
            pltpu.VMEM((2 * tile, d_attn), BF16),
            pltpu.VMEM((2 * tile, V7X_LANES), BF16),
            pltpu.VMEM((POOL_HALO + tile, d_pool), F32),
            pltpu.VMEM((tile, d_attn + d_pool), BF16),
        ],
        compiler_params=pltpu.CompilerParams(
            dimension_semantics=("parallel", "arbitrary"), vmem_limit_bytes=MIX_VMEM_LIMIT),
        name="mix",
    )(h, z, z, z, z, z, z, z, bias, w_pool, pool_scale, w_out)


def _bias_body(base_ref, o_ref):
    heads = base_ref.shape[0]
    width = base_ref.shape[2]
    qi = lax.broadcasted_iota(jnp.int32, (Q_BLOCK, KEY_BAND), 0)
    kb = lax.broadcasted_iota(jnp.int32, (Q_BLOCK, KEY_BAND), 1)
    q_chunk_start = qi & ~(CHUNK - 1)
    valid = (kb >= q_chunk_start) & (kb < q_chunk_start + LEFT + CHUNK)
    for h in range(heads):
        row = jnp.broadcast_to(base_ref[h], (Q_BLOCK, width))
        toep = pltpu.roll(row, 0, 1, stride=1, stride_axis=0)
        o_ref[h // 2, (h % 2) * Q_BLOCK:(h % 2 + 1) * Q_BLOCK, :] = jnp.where(
            valid, toep[:, 0:KEY_BAND], NEG_INF)


def _attn_bias(rel_table):
    heads = rel_table.shape[0]
    assert KEY_BAND - 1 - LEFT <= REL_CLIP
    far = rel_table[:, 2 * REL_CLIP:]
    base = jnp.concatenate([
        jnp.broadcast_to(far, (heads, LEFT - REL_CLIP + 1)),
        rel_table[:, 2 * REL_CLIP - 1:REL_CLIP + LEFT - KEY_BAND:-1],
        jnp.broadcast_to(far, (heads, Q_BLOCK)),
    ], axis=1) * LOG2_E
    width = KEY_BAND + Q_BLOCK
    assert base.shape == (heads, width)
    return pl.pallas_call(
        _bias_body,
        out_shape=jax.ShapeDtypeStruct((heads // 2, 2 * Q_BLOCK, KEY_BAND), F32),
        name="attn_bias",
    )(base.reshape(heads, 1, width))


def _cross_body(h_ref, g_ref, wq_ref, k_ref, v_ref, wo_ref, o_ref, oc_ref):
    h = h_ref[...]
    xn = _rms(h, g_ref[...]).astype(BF16)
    q = jnp.dot(xn, wq_ref[...], preferred_element_type=F32).astype(BF16)
    scale = CROSS_HEAD_DIM ** -0.5
    for hh in range(CROSS_HEADS):
        cols = slice(hh * CROSS_HEAD_DIM, (hh + 1) * CROSS_HEAD_DIM)
        s = lax.dot_general(q[:, cols], k_ref[:, cols], _NT, preferred_element_type=F32) * scale
        m = jnp.max(s, axis=-1, keepdims=True)
        e = jnp.exp(s - m)
        l = jnp.sum(e, axis=-1, keepdims=True)
        o = jnp.dot(e.astype(BF16), v_ref[:, cols], preferred_element_type=F32) / l
        oc_ref[:, cols] = o.astype(BF16)
    o_ref[...] = h + jnp.dot(oc_ref[...], wo_ref[...], preferred_element_type=F32)


def _cross(h, gain, w_cq, kv, w_co, *, seq, n_mem):
    t, d = h.shape
    dc = w_cq.shape[1]
    tps = seq // TOKEN_TILE
    return pl.pallas_call(
        _cross_body,
        grid=(t // TOKEN_TILE,),
        in_specs=[
            pl.BlockSpec((TOKEN_TILE, d), lambda i: (i, 0)),
            _const_spec((1, d)),
            _const_spec(w_cq.shape),
            pl.BlockSpec((n_mem, dc), lambda i: (i // tps, 0)),
            pl.BlockSpec((n_mem, dc), lambda i: (i // tps, 1)),
            _const_spec(w_co.shape),
        ],
        out_specs=pl.BlockSpec((TOKEN_TILE, d), lambda i: (i, 0)),
        out_shape=jax.ShapeDtypeStruct((t, d), F32),
        scratch_shapes=[pltpu.VMEM((TOKEN_TILE, dc), BF16)],
        compiler_params=pltpu.CompilerParams(
            dimension_semantics=("parallel",), vmem_limit_bytes=VMEM_LIMIT),
        name="cross",
    )(h, gain, w_cq, kv, kv, w_co)


def kernel(x, mem, ffn1_norm, ffn1_w_gate, ffn1_w_up, ffn1_w_down, mix_norm, w_in, rel_bias,
           w_pool, pool_scale, w_out, cross_norm, mem_norm, w_cq, w_ckv, w_co, ffn2_norm,
           ffn2_w_gate, ffn2_w_up, ffn2_w_down, final_norm):
    batch, seq, d = x.shape
    n_mem = mem.shape[1]
    depth = ffn1_norm.shape[0]
    d_attn = ATTN_HEADS * ATTN_HEAD_DIM
    assert seq % TOKEN_TILE == 0 and TOKEN_TILE % Q_BLOCK == 0 and TOKEN_TILE >= LEFT

    h = x.reshape(batch * seq, d)
    mem2 = mem.reshape(batch * n_mem, d)
    fin = final_norm.reshape(1, d)
    qscale = jnp.where(jnp.arange(w_in.shape[-1]) < d_attn,
                       ATTN_HEAD_DIM ** -0.5 * LOG2_E, 1.0).astype(F32)

    for l in range(depth):
        last = l == depth - 1
        h = _ffn(h, ffn1_norm[l].reshape(1, d), ffn1_w_gate[l].astype(BF16),
                 ffn1_w_up[l].astype(BF16), (FFN_RES_WEIGHT * ffn1_w_down[l]).astype(BF16), fin,
                 final_norm=False)
        z = _proj(h, mix_norm[l].reshape(1, d), (w_in[l] * qscale).astype(BF16),
                  n_split=4, name="proj")
        h = _mix(h, z, _attn_bias(rel_bias[l]), w_pool[l].astype(BF16),
                 pool_scale[l].reshape(1, -1), w_out[l].astype(BF16), batch=batch, seq=seq)
        kv = _proj(mem2, mem_norm[l].reshape(1, d), w_ckv[l].astype(BF16), n_split=1, name="memkv")
        h = _cross(h, cross_norm[l].reshape(1, d), w_cq[l].astype(BF16), kv,
                   w_co[l].astype(BF16), seq=seq, n_mem=n_mem)
        h = _ffn(h, ffn2_norm[l].reshape(1, d), ffn2_w_gate[l].astype(BF16),
                 ffn2_w_up[l].astype(BF16), (FFN_RES_WEIGHT * ffn2_w_down[l]).astype(BF16), fin,
                 final_norm=last)
    return h.reshape(batch, seq, d)
```

```python
import functools

import jax
import jax.numpy as jnp
from jax import lax
from jax.experimental import pallas as pl
from jax.experimental.pallas import tpu as pltpu

F32 = jnp.float32
BF16 = jnp.bfloat16

EPS = 1e-6
NEG_INF = -1e30
FFN_RES_WEIGHT = 0.5
LOG2_E = 1.4426950408889634

CHUNK = 64
LEFT_CHUNKS = 8
REL_CLIP = 128
ATTN_HEADS = 16
ATTN_HEAD_DIM = 64
POOL_WINDOWS = (2, 4, 8, 16)
CROSS_HEADS = 4
CROSS_HEAD_DIM = 128

V7X_LANES = 128
V7X_VMEM_BYTES = 64 * 1024 * 1024
V7X_VMEM_RESERVE = 2 * 1024 * 1024

TOKEN_TILE = 512
FFN_TOKEN_TILE = 1024
FF_TILE = 512
FF_SUBCHUNKS = 2
Q_BLOCK = 2 * CHUNK
LEFT = LEFT_CHUNKS * CHUNK
KEY_BAND = LEFT + Q_BLOCK
POOL_HALO = 16
TEMP_VMEM_BYTES = 11 * 1024 * 1024

_NT = (((1,), (1,)), ((), ()))


def _nbytes(shape, dtype):
    n = 1
    for s in shape:
        n *= s
    return n * jnp.dtype(dtype).itemsize


def _vmem_limit(window_bytes):
    return min(window_bytes + TEMP_VMEM_BYTES, V7X_VMEM_BYTES - V7X_VMEM_RESERVE)


def _rms(x):
    ms = jnp.mean(x * x, axis=-1, keepdims=True)
    return x * lax.rsqrt(ms + EPS)


def _const_spec(shape):
    nd = len(shape)
    return pl.BlockSpec(shape, lambda *_: (0,) * nd, pipeline_mode=pl.Buffered(1))


def _ffn_body(x_ref, wg_ref, wu_ref, wd_ref, fin_ref, o_ref, xn_ref, *, final_norm):
    k = pl.program_id(1)

    @pl.when(k == 0)
    def _():
        x = x_ref[...]
        xn_ref[...] = _rms(x).astype(BF16)
        o_ref[...] = x

    xn = xn_ref[...]
    sub = wg_ref.shape[1] // FF_SUBCHUNKS
    gu = []
    for c in range(FF_SUBCHUNKS):
        cols = slice(c * sub, (c + 1) * sub)
        gu.append((jnp.dot(xn, wg_ref[:, cols], preferred_element_type=F32),
                   jnp.dot(xn, wu_ref[:, cols], preferred_element_type=F32)))
    acc = None
    for c, (g, u) in enumerate(gu):
        a = ((g * jax.nn.sigmoid(g)) * u).astype(BF16)
        d = jnp.dot(a, wd_ref[c * sub:(c + 1) * sub, :], preferred_element_type=F32)
        acc = d if acc is None else acc + d
    o_ref[...] += acc

    if final_norm:
        @pl.when(k == pl.num_programs(1) - 1)
        def _():
            o_ref[...] = _rms(o_ref[...]) * fin_ref[...]


def _ffn(h, wg, wu, wd_half, fin, *, final_norm):
    t, d = h.shape
    dff = wg.shape[1]
    tm = FFN_TOKEN_TILE
    windows = (2 * 2 * _nbytes((tm, d), F32) + 2 * 3 * _nbytes((d, FF_TILE), BF16)
               + _nbytes((tm, d), BF16))
    return pl.pallas_call(
        functools.partial(_ffn_body, final_norm=final_norm),
        grid=(t // tm, dff // FF_TILE),
        in_specs=[
            pl.BlockSpec((tm, d), lambda i, k: (i, 0)),
            pl.BlockSpec((d, FF_TILE), lambda i, k: (0, k)),
            pl.BlockSpec((d, FF_TILE), lambda i, k: (0, k)),
            pl.BlockSpec((FF_TILE, d), lambda i, k: (k, 0)),
            pl.BlockSpec((1, d), lambda i, k: (0, 0)),
        ],
        out_specs=pl.BlockSpec((tm, d), lambda i, k: (i, 0)),
        out_shape=jax.ShapeDtypeStruct((t, d), F32),
        scratch_shapes=[pltpu.VMEM((tm, d), BF16)],
        compiler_params=pltpu.CompilerParams(
            dimension_semantics=("parallel", "arbitrary"),
            vmem_limit_bytes=_vmem_limit(windows)),
        name="ffn_final" if final_norm else "ffn",
    )(h, wg, wu, wd_half, fin)


def _proj_body(h_ref, w_ref, z_ref, *, n_split):
    xn = _rms(h_ref[...]).astype(BF16)
    n = w_ref.shape[1] // n_split
    for c in range(n_split):
        z = jnp.dot(xn, w_ref[:, c * n:(c + 1) * n], preferred_element_type=F32)
        z_ref[:, c * n:(c + 1) * n] = z.astype(z_ref.dtype)


def _proj(h, w, *, n_split, name):
    t, d = h.shape
    n = w.shape[1]
    windows = (2 * _nbytes((TOKEN_TILE, d), F32) + 2 * _nbytes((TOKEN_TILE, n), BF16)
               + _nbytes((d, n), BF16))
    return pl.pallas_call(
        functools.partial(_proj_body, n_split=n_split),
        grid=(t // TOKEN_TILE,),
        in_specs=[
            pl.BlockSpec((TOKEN_TILE, d), lambda i: (i, 0)),
            _const_spec((d, n)),
        ],
        out_specs=pl.BlockSpec((TOKEN_TILE, n), lambda i: (i, 0)),
        out_shape=jax.ShapeDtypeStruct((t, n), BF16),
        compiler_params=pltpu.CompilerParams(
            dimension_semantics=("parallel",), vmem_limit_bytes=_vmem_limit(windows)),
        name=name,
    )(h, w)


def _mix_body(h_ref, q_ref, kp_ref, kc_ref, vp_ref, vc_ref, uh_ref, uc_ref, bias_ref,
              pb_ref, ph_ref, wp_ref, ps_ref, wo_ref, o_ref, kk_ref, vv_ref, y_ref):
    j = pl.program_id(1)
    tile = q_ref.shape[0]
    d_attn = q_ref.shape[1]
    n_pairs = d_attn // V7X_LANES

    kk_ref[0:tile, :] = kp_ref[...]
    kk_ref[tile:2 * tile, :] = kc_ref[...]
    vv_ref[0:tile, :] = vp_ref[...]
    vv_ref[tile:2 * tile, :] = vc_ref[...]

    lane = lax.broadcasted_iota(jnp.int32, (Q_BLOCK, V7X_LANES), 1)
    low = lane < ATTN_HEAD_DIM
    ones_cols = jnp.ones((KEY_BAND, V7X_LANES), BF16)
    kbi = lax.broadcasted_iota(jnp.int32, (1, KEY_BAND), 1)

    def block(i, carry):
        r0 = pl.multiple_of(i * Q_BLOCK, Q_BLOCK)
        b0 = pl.multiple_of(tile - LEFT + r0, Q_BLOCK)
        startmask = jnp.where(j * tile + r0 - LEFT + kbi >= 0, 0.0, NEG_INF).astype(F32)

        def scores(p):
            cols = slice(p * V7X_LANES, (p + 1) * V7X_LANES)
            qb = q_ref[pl.ds(r0, Q_BLOCK), cols]
            zero = jnp.zeros_like(qb)
            qbd = jnp.concatenate([jnp.where(low, qb, zero), jnp.where(low, zero, qb)], axis=0)
            return lax.dot_general(qbd, kk_ref[pl.ds(b0, KEY_BAND), cols], _NT,
                                   preferred_element_type=F32)

        s_all = [scores(p) + (bias_ref[p] + startmask) for p in range(n_pairs)]
        m_all = [jnp.max(s, axis=-1, keepdims=True) for s in s_all]
        e_all = [jnp.exp2(s - m).astype(BF16) for s, m in zip(s_all, m_all)]
        for p in range(n_pairs):
            cols = slice(p * V7X_LANES, (p + 1) * V7X_LANES)
            rhs = jnp.concatenate([vv_ref[pl.ds(b0, KEY_BAND), cols], ones_cols], axis=1)
            ol = jnp.dot(e_all[p], rhs, preferred_element_type=F32)
            o = ol[:, 0:V7X_LANES] / ol[:, V7X_LANES:2 * V7X_LANES]
            op = jnp.where(low, o[0:Q_BLOCK], o[Q_BLOCK:2 * Q_BLOCK])
            y_ref[pl.ds(r0, Q_BLOCK), cols] = op.astype(BF16)
        return carry

    lax.fori_loop(0, tile // Q_BLOCK, block, 0)

    uh = uh_ref[...]
    uh = jnp.where(j == 0, jnp.zeros_like(uh), uh)
    t_glob = lax.broadcasted_iota(jnp.int32, (tile, 1), 0) + j * tile
    gdim = uc_ref.shape[1] // len(POOL_WINDOWS)

    def window_sum(g):
        cols = slice(g * gdim, (g + 1) * gdim)
        wsum = jnp.dot(pb_ref[g], uc_ref[:, cols], preferred_element_type=F32)
        head = wsum[0:POOL_HALO] + jnp.dot(ph_ref[g], uh[:, cols], preferred_element_type=F32)
        return jnp.concatenate([head, wsum[POOL_HALO:]], axis=0)

    def delta(g, wsum):
        cols = slice(g * gdim, (g + 1) * gdim)
        cnt = jnp.minimum(t_glob + 1, POOL_WINDOWS[g]).astype(F32)
        return (wsum / cnt - uc_ref[:, cols].astype(F32)).astype(BF16)

    def pool_out(g, dlt):
        cols = slice(g * gdim, (g + 1) * gdim)
        yp = jnp.dot(dlt, wp_ref[g], preferred_element_type=F32) * ps_ref[:, cols]
        y_ref[:, d_attn + g * gdim:d_attn + (g + 1) * gdim] = yp.astype(BF16)

    def project_attn(c, n_chunks):
        n = o_ref.shape[1] // n_chunks
        cols = slice(c * n, (c + 1) * n)
        o_ref[:, cols] = h_ref[:, cols] + jnp.dot(
            y_ref[:, 0:d_attn], wo_ref[0:d_attn, cols], preferred_element_type=F32)

    n_groups = len(POOL_WINDOWS)
    wsums = [window_sum(g) for g in range(n_groups)]
    project_attn(0, n_groups)
    for g in range(n_groups):
        pool_out(g, delta(g, wsums[g]))
    for c in range(1, n_groups):
        project_attn(c, n_groups)
    d_mix = y_ref.shape[1]
    o_ref[...] += jnp.dot(y_ref[:, d_attn:d_mix], wo_ref[d_attn:d_mix, :],
                          preferred_element_type=F32)


def _pool_bands(tile):
    t = jnp.arange(tile)[:, None]
    c = jnp.arange(tile)[None, :]
    th = jnp.arange(POOL_HALO)[:, None]
    ch = jnp.arange(POOL_HALO)[None, :] - POOL_HALO
    inner = [((c <= t) & (c > t - w)) for w in POOL_WINDOWS]
    halo = [(ch > th - w) for w in POOL_WINDOWS]
    return jnp.stack(inner).astype(BF16), jnp.stack(halo).astype(BF16)


def _mix(h, z, bias, w_pool, pool_scale, w_out, *, batch, seq):
    t, d = h.shape
    d_attn = ATTN_HEADS * ATTN_HEAD_DIM
    d_pool = z.shape[1] - 3 * d_attn
    assert d_attn == d_pool and max(POOL_WINDOWS) - 1 <= POOL_HALO
    tile = TOKEN_TILE
    tps = seq // tile
    hpt = tile // POOL_HALO
    pband, phalo = _pool_bands(tile)

    def cur(col):
        return pl.BlockSpec((tile, d_attn), lambda b, j: (b * tps + j, col))

    def prev(col):
        return pl.BlockSpec((tile, d_attn), lambda b, j: (jnp.maximum(b * tps + j - 1, 0), col))

    halo_cols = d_pool // d_attn * 3
    in_specs = [
        pl.BlockSpec((tile, d), lambda b, j: (b * tps + j, 0)),
        cur(0),
        prev(1), cur(1),
        prev(2), cur(2),
        pl.BlockSpec((POOL_HALO, d_pool),
                     lambda b, j: (jnp.maximum((b * tps + j) * hpt - 1, 0), halo_cols)),
        cur(3),
        _const_spec(bias.shape),
        _const_spec(pband.shape),
        _const_spec(phalo.shape),
        _const_spec(w_pool.shape),
        _const_spec(pool_scale.shape),
        _const_spec(w_out.shape),
    ]
    scratch = [
        ((2 * tile, d_attn), BF16),
        ((2 * tile, d_attn), BF16),
        ((tile, d_attn + d_pool), BF16),
    ]
    windows = (2 * 2 * _nbytes((tile, d), F32) + 2 * 6 * _nbytes((tile, d_attn), BF16)
               + _nbytes(bias.shape, F32) + _nbytes(pband.shape, BF16)
               + _nbytes(w_pool.shape, BF16) + _nbytes(w_out.shape, BF16)
               + sum(_nbytes(s, dt) for s, dt in scratch))
    return pl.pallas_call(
        _mix_body,
        grid=(batch, tps),
        in_specs=in_specs,
        out_specs=pl.BlockSpec((tile, d), lambda b, j: (b * tps + j, 0)),
        out_shape=jax.ShapeDtypeStruct((t, d), F32),
        scratch_shapes=[pltpu.VMEM(s, dt) for s, dt in scratch],
        compiler_params=pltpu.CompilerParams(
            dimension_semantics=("parallel", "arbitrary"),
            vmem_limit_bytes=_vmem_limit(windows)),
        name="mix",
    )(h, z, z, z, z, z, z, z, bias, pband, phalo, w_pool, pool_scale, w_out)


def _bias_body(base_ref, o_ref):
    heads = base_ref.shape[0]
    width = base_ref.shape[2]
    qi = lax.broadcasted_iota(jnp.int32, (Q_BLOCK, KEY_BAND), 0)
    kb = lax.broadcasted_iota(jnp.int32, (Q_BLOCK, KEY_BAND), 1)
    q_chunk_start = qi & ~(CHUNK - 1)
    valid = (kb >= q_chunk_start) & (kb < q_chunk_start + LEFT + CHUNK)
    for h in range(heads):
        row = jnp.broadcast_to(base_ref[h], (Q_BLOCK, width))
        toep = pltpu.roll(row, 0, 1, stride=1, stride_axis=0)
        o_ref[h // 2, (h % 2) * Q_BLOCK:(h % 2 + 1) * Q_BLOCK, :] = jnp.where(
            valid, toep[:, 0:KEY_BAND], NEG_INF)


def _attn_bias(rel_table):
    heads = rel_table.shape[0]
    assert KEY_BAND - 1 - LEFT <= REL_CLIP
    far = rel_table[:, 2 * REL_CLIP:]
    base = jnp.concatenate([
        jnp.broadcast_to(far, (heads, LEFT - REL_CLIP + 1)),
        rel_table[:, 2 * REL_CLIP - 1:REL_CLIP + LEFT - KEY_BAND:-1],
        jnp.broadcast_to(far, (heads, Q_BLOCK)),
    ], axis=1) * LOG2_E
    width = KEY_BAND + Q_BLOCK
    assert base.shape == (heads, width)
    return pl.pallas_call(
        _bias_body,
        out_shape=jax.ShapeDtypeStruct((heads // 2, 2 * Q_BLOCK, KEY_BAND), F32),
        name="attn_bias",
    )(base.reshape(heads, 1, width))


def _cross_body(h_ref, wq_ref, k_ref, v_ref, wo_ref, o_ref, oc_ref):
    h = h_ref[...]
    xn = _rms(h).astype(BF16)
    q = jnp.dot(xn, wq_ref[...], preferred_element_type=F32).astype(BF16)
    scale = CROSS_HEAD_DIM ** -0.5
    for hh in range(CROSS_HEADS):
        cols = slice(hh * CROSS_HEAD_DIM, (hh + 1) * CROSS_HEAD_DIM)
        s = lax.dot_general(q[:, cols], k_ref[:, cols], _NT, preferred_element_type=F32) * scale
        m = jnp.max(s, axis=-1, keepdims=True)
        e = jnp.exp(s - m)
        l = jnp.sum(e, axis=-1, keepdims=True)
        o = jnp.dot(e.astype(BF16), v_ref[:, cols], preferred_element_type=F32) / l
        oc_ref[:, cols] = o.astype(BF16)
    o_ref[...] = h + jnp.dot(oc_ref[...], wo_ref[...], preferred_element_type=F32)


def _cross(h, w_cq, kv, w_co, *, seq, n_mem):
    t, d = h.shape
    dc = w_cq.shape[1]
    tps = seq // TOKEN_TILE
    windows = (2 * 2 * _nbytes((TOKEN_TILE, d), F32) + 2 * 2 * _nbytes((n_mem, dc), BF16)
               + 2 * _nbytes((d, dc), BF16) + _nbytes((TOKEN_TILE, dc), BF16))
    return pl.pallas_call(
        _cross_body,
        grid=(t // TOKEN_TILE,),
        in_specs=[
            pl.BlockSpec((TOKEN_TILE, d), lambda i: (i, 0)),
            _const_spec(w_cq.shape),
            pl.BlockSpec((n_mem, dc), lambda i: (i // tps, 0)),
            pl.BlockSpec((n_mem, dc), lambda i: (i // tps, 1)),
            _const_spec(w_co.shape),
        ],
        out_specs=pl.BlockSpec((TOKEN_TILE, d), lambda i: (i, 0)),
        out_shape=jax.ShapeDtypeStruct((t, d), F32),
        scratch_shapes=[pltpu.VMEM((TOKEN_TILE, dc), BF16)],
        compiler_params=pltpu.CompilerParams(
            dimension_semantics=("parallel",), vmem_limit_bytes=_vmem_limit(windows)),
        name="cross",
    )(h, w_cq, kv, kv, w_co)


def _fold(gain, w, col_scale=None):
    w = gain[:, None] * w
    if col_scale is not None:
        w = w * col_scale
    return w.astype(BF16)


def kernel(x, mem, ffn1_norm, ffn1_w_gate, ffn1_w_up, ffn1_w_down, mix_norm, w_in, rel_bias,
           w_pool, pool_scale, w_out, cross_norm, mem_norm, w_cq, w_ckv, w_co, ffn2_norm,
           ffn2_w_gate, ffn2_w_up, ffn2_w_down, final_norm):
    batch, seq, d = x.shape
    n_mem = mem.shape[1]
    depth = ffn1_norm.shape[0]
    d_attn = ATTN_HEADS * ATTN_HEAD_DIM
    assert seq % TOKEN_TILE == 0 and TOKEN_TILE % Q_BLOCK == 0 and TOKEN_TILE >= LEFT

    h = x.reshape(batch * seq, d)
    mem2 = mem.reshape(batch * n_mem, d)
    fin = final_norm.reshape(1, d)
    qscale = jnp.where(jnp.arange(w_in.shape[-1]) < d_attn,
                       ATTN_HEAD_DIM ** -0.5 * LOG2_E, 1.0).astype(F32)

    for l in range(depth):
        last = l == depth - 1
        h = _ffn(h, _fold(ffn1_norm[l], ffn1_w_gate[l]), _fold(ffn1_norm[l], ffn1_w_up[l]),
                 (FFN_RES_WEIGHT * ffn1_w_down[l]).astype(BF16), fin, final_norm=False)
        z = _proj(h, _fold(mix_norm[l], w_in[l], qscale), n_split=4, name="proj")
        h = _mix(h, z, _attn_bias(rel_bias[l]), w_pool[l].astype(BF16),
                 pool_scale[l].reshape(1, -1), w_out[l].astype(BF16), batch=batch, seq=seq)
        kv = _proj(mem2, _fold(mem_norm[l], w_ckv[l]), n_split=1, name="memkv")
        h = _cross(h, _fold(cross_norm[l], w_cq[l]), kv, w_co[l].astype(BF16),
                   seq=seq, n_mem=n_mem)
        h = _ffn(h, _fold(ffn2_norm[l], ffn2_w_gate[l]), _fold(ffn2_norm[l], ffn2_w_up[l]),
                 (FFN_RES_WEIGHT * ffn2_w_down[l]).astype(BF16), fin, final_norm=last)
    return h.reshape(batch, seq, d)
```

```python
import functools

import jax
import jax.numpy as jnp
from jax import lax
from jax.experimental import pallas as pl
from jax.experimental.pallas import tpu as pltpu

F32 = jnp.float32
BF16 = jnp.bfloat16

EPS = 1e-6
NEG_INF = -1e30
FFN_RES_WEIGHT = 0.5
LOG2_E = 1.4426950408889634

CHUNK = 64
LEFT_CHUNKS = 8
REL_CLIP = 128
ATTN_HEADS = 16
ATTN_HEAD_DIM = 64
POOL_WINDOWS = (2, 4, 8, 16)
CROSS_HEADS = 4
CROSS_HEAD_DIM = 128

V7X_LANES = 128
V7X_VMEM_BYTES = 64 * 1024 * 1024
V7X_VMEM_RESERVE = 2 * 1024 * 1024

TOKEN_TILE = 512
FFN_TOKEN_TILE = 1024
FF_TILE = 512
FF_SUBCHUNKS = 2
Q_BLOCK = 2 * CHUNK
LEFT = LEFT_CHUNKS * CHUNK
KEY_BAND = LEFT + Q_BLOCK
SCORE_LEAD = 3
POOL_HALO = 16
TEMP_VMEM_BYTES = 11 * 1024 * 1024

_NT = (((1,), (1,)), ((), ()))


def _nbytes(shape, dtype):
    n = 1
    for s in shape:
        n *= s
    return n * jnp.dtype(dtype).itemsize


def _vmem_limit(window_bytes):
    return min(window_bytes + TEMP_VMEM_BYTES, V7X_VMEM_BYTES - V7X_VMEM_RESERVE)


def _rms(x):
    ms = jnp.mean(x * x, axis=-1, keepdims=True)
    return x * lax.rsqrt(ms + EPS)


def _const_spec(shape):
    nd = len(shape)
    return pl.BlockSpec(shape, lambda *_: (0,) * nd, pipeline_mode=pl.Buffered(1))


def _ffn_body(x_ref, wg_ref, wu_ref, wd_ref, fin_ref, o_ref, xn_ref, *, final_norm):
    k = pl.program_id(1)

    @pl.when(k == 0)
    def _():
        x = x_ref[...]
        xn_ref[...] = _rms(x).astype(BF16)
        o_ref[...] = x

    xn = xn_ref[...]
    sub = wg_ref.shape[1] // FF_SUBCHUNKS
    gu = []
    for c in range(FF_SUBCHUNKS):
        cols = slice(c * sub, (c + 1) * sub)
        gu.append((jnp.dot(xn, wg_ref[:, cols], preferred_element_type=F32),
                   jnp.dot(xn, wu_ref[:, cols], preferred_element_type=F32)))
    acc = None
    for c, (g, u) in enumerate(gu):
        a = ((g * jax.nn.sigmoid(g)) * u).astype(BF16)
        d = jnp.dot(a, wd_ref[c * sub:(c + 1) * sub, :], preferred_element_type=F32)
        acc = d if acc is None else acc + d
    o_ref[...] += acc

    if final_norm:
        @pl.when(k == pl.num_programs(1) - 1)
        def _():
            o_ref[...] = _rms(o_ref[...]) * fin_ref[...]


def _ffn(h, wg, wu, wd_half, fin, *, final_norm):
    t, d = h.shape
    dff = wg.shape[1]
    tm = FFN_TOKEN_TILE
    windows = (2 * 2 * _nbytes((tm, d), F32) + 2 * 3 * _nbytes((d, FF_TILE), BF16)
               + _nbytes((tm, d), BF16))
    return pl.pallas_call(
        functools.partial(_ffn_body, final_norm=final_norm),
        grid=(t // tm, dff // FF_TILE),
        in_specs=[
            pl.BlockSpec((tm, d), lambda i, k: (i, 0)),
            pl.BlockSpec((d, FF_TILE), lambda i, k: (0, k)),
            pl.BlockSpec((d, FF_TILE), lambda i, k: (0, k)),
            pl.BlockSpec((FF_TILE, d), lambda i, k: (k, 0)),
            pl.BlockSpec((1, d), lambda i, k: (0, 0)),
        ],
        out_specs=pl.BlockSpec((tm, d), lambda i, k: (i, 0)),
        out_shape=jax.ShapeDtypeStruct((t, d), F32),
        scratch_shapes=[pltpu.VMEM((tm, d), BF16)],
        compiler_params=pltpu.CompilerParams(
            dimension_semantics=("parallel", "arbitrary"),
            vmem_limit_bytes=_vmem_limit(windows)),
        name="ffn_final" if final_norm else "ffn",
    )(h, wg, wu, wd_half, fin)


def _proj_body(h_ref, w_ref, z_ref, *, n_split):
    half = h_ref.shape[0] // 2
    xn = [_rms(h_ref[r:r + half, :]).astype(BF16) for r in (0, half)]
    n = w_ref.shape[1] // n_split
    for r, x in zip((0, half), xn):
        z = jnp.dot(x, w_ref[:, 0:n], preferred_element_type=F32)
        z_ref[r:r + half, 0:n] = z.astype(z_ref.dtype)
    xn = jnp.concatenate(xn, axis=0)
    for c in range(1, n_split):
        z = jnp.dot(xn, w_ref[:, c * n:(c + 1) * n], preferred_element_type=F32)
        z_ref[:, c * n:(c + 1) * n] = z.astype(z_ref.dtype)


def _proj(h, w, *, n_split, name):
    t, d = h.shape
    n = w.shape[1]
    windows = (2 * _nbytes((TOKEN_TILE, d), F32) + 2 * _nbytes((TOKEN_TILE, n), BF16)
               + _nbytes((d, n), BF16))
    return pl.pallas_call(
        functools.partial(_proj_body, n_split=n_split),
        grid=(t // TOKEN_TILE,),
        in_specs=[
            pl.BlockSpec((TOKEN_TILE, d), lambda i: (i, 0)),
            _const_spec((d, n)),
        ],
        out_specs=pl.BlockSpec((TOKEN_TILE, n), lambda i: (i, 0)),
        out_shape=jax.ShapeDtypeStruct((t, n), BF16),
        compiler_params=pltpu.CompilerParams(
            dimension_semantics=("parallel",), vmem_limit_bytes=_vmem_limit(windows)),
        name=name,
    )(h, w)


def _mix_body(h_ref, q_ref, kp_ref, kc_ref, vp_ref, vc_ref, uh_ref, uc_ref, bias_ref,
              pb_ref, ph_ref, wp_ref, ps_ref, wo_ref, o_ref, kk_ref, vv_ref, y_ref):
    j = pl.program_id(1)
    tile = q_ref.shape[0]
    d_attn = q_ref.shape[1]
    n_pairs = d_attn // V7X_LANES

    kk_ref[0:tile, :] = kp_ref[...]
    kk_ref[tile:2 * tile, :] = kc_ref[...]
    vv_ref[0:tile, :] = vp_ref[...]
    vv_ref[tile:2 * tile, :] = vc_ref[...]

    lane = lax.broadcasted_iota(jnp.int32, (Q_BLOCK, V7X_LANES), 1)
    low = lane < ATTN_HEAD_DIM
    ones_cols = jnp.ones((KEY_BAND, V7X_LANES), BF16)
    kbi = lax.broadcasted_iota(jnp.int32, (1, KEY_BAND), 1)

    def block(i, carry):
        r0 = pl.multiple_of(i * Q_BLOCK, Q_BLOCK)
        b0 = pl.multiple_of(tile - LEFT + r0, Q_BLOCK)
        startmask = jnp.where(j * tile + r0 - LEFT + kbi >= 0, 0.0, NEG_INF).astype(F32)

        def scores(p):
            cols = slice(p * V7X_LANES, (p + 1) * V7X_LANES)
            qb = q_ref[pl.ds(r0, Q_BLOCK), cols]
            zero = jnp.zeros_like(qb)
            qbd = jnp.concatenate([jnp.where(low, qb, zero), jnp.where(low, zero, qb)], axis=0)
            return lax.dot_general(qbd, kk_ref[pl.ds(b0, KEY_BAND), cols], _NT,
                                   preferred_element_type=F32)

        def biased_scores(p):
            return scores(p) + (bias_ref[p] + startmask)

        s_all = [biased_scores(p) for p in range(SCORE_LEAD)]
        for p in range(n_pairs):
            s = s_all[p]
            m = jnp.max(s, axis=-1, keepdims=True)
            e = jnp.exp2(s - m).astype(BF16)
            if p + SCORE_LEAD < n_pairs:
                s_all.append(biased_scores(p + SCORE_LEAD))
            cols = slice(p * V7X_LANES, (p + 1) * V7X_LANES)
            rhs = jnp.concatenate([vv_ref[pl.ds(b0, KEY_BAND), cols], ones_cols], axis=1)
            ol = jnp.dot(e, rhs, preferred_element_type=F32)
            o = ol[:, 0:V7X_LANES] / ol[:, V7X_LANES:2 * V7X_LANES]
            op = jnp.where(low, o[0:Q_BLOCK], o[Q_BLOCK:2 * Q_BLOCK])
            y_ref[pl.ds(r0, Q_BLOCK), cols] = op.astype(BF16)
        return carry

    lax.fori_loop(0, tile // Q_BLOCK, block, 0, unroll=True)

    uh = uh_ref[...]
    uh = jnp.where(j == 0, jnp.zeros_like(uh), uh)
    t_glob = lax.broadcasted_iota(jnp.int32, (tile, 1), 0) + j * tile
    gdim = uc_ref.shape[1] // len(POOL_WINDOWS)

    def window_sum(g):
        cols = slice(g * gdim, (g + 1) * gdim)
        wsum = jnp.dot(pb_ref[g], uc_ref[:, cols], preferred_element_type=F32)
        head = wsum[0:POOL_HALO] + jnp.dot(ph_ref[g], uh[:, cols], preferred_element_type=F32)
        return jnp.concatenate([head, wsum[POOL_HALO:]], axis=0)

    def delta(g, wsum):
        cols = slice(g * gdim, (g + 1) * gdim)
        cnt = jnp.minimum(t_glob + 1, POOL_WINDOWS[g]).astype(F32)
        return (wsum / cnt - uc_ref[:, cols].astype(F32)).astype(BF16)

    def pool_out(g, dlt):
        cols = slice(g * gdim, (g + 1) * gdim)
        yp = jnp.dot(dlt, wp_ref[g], preferred_element_type=F32) * ps_ref[:, cols]
        y_ref[:, d_attn + g * gdim:d_attn + (g + 1) * gdim] = yp.astype(BF16)

    def project_attn(c, n_chunks):
        n = o_ref.shape[1] // n_chunks
        cols = slice(c * n, (c + 1) * n)
        o_ref[:, cols] = h_ref[:, cols] + jnp.dot(
            y_ref[:, 0:d_attn], wo_ref[0:d_attn, cols], preferred_element_type=F32)

    n_groups = len(POOL_WINDOWS)
    wsums = [window_sum(g) for g in range(n_groups)]
    project_attn(0, n_groups)
    for g in range(n_groups):
        pool_out(g, delta(g, wsums[g]))
    for c in range(1, n_groups):
        project_attn(c, n_groups)
    d_mix = y_ref.shape[1]
    o_ref[...] += jnp.dot(y_ref[:, d_attn:d_mix], wo_ref[d_attn:d_mix, :],
                          preferred_element_type=F32)


def _pool_bands(tile):
    t = jnp.arange(tile)[:, None]
    c = jnp.arange(tile)[None, :]
    th = jnp.arange(POOL_HALO)[:, None]
    ch = jnp.arange(POOL_HALO)[None, :] - POOL_HALO
    inner = [((c <= t) & (c > t - w)) for w in POOL_WINDOWS]
    halo = [(ch > th - w) for w in POOL_WINDOWS]
    return jnp.stack(inner).astype(BF16), jnp.stack(halo).astype(BF16)


def _mix(h, z, bias, w_pool, pool_scale, w_out, *, batch, seq):
    t, d = h.shape
    d_attn = ATTN_HEADS * ATTN_HEAD_DIM
    d_pool = z.shape[1] - 3 * d_attn
    assert d_attn == d_pool and max(POOL_WINDOWS) - 1 <= POOL_HALO
    tile = TOKEN_TILE
    tps = seq // tile
    hpt = tile // POOL_HALO
    pband, phalo = _pool_bands(tile)

    def cur(col):
        return pl.BlockSpec((tile, d_attn), lambda b, j: (b * tps + j, col))

    def prev(col):
        return pl.BlockSpec((tile, d_attn), lambda b, j: (jnp.maximum(b * tps + j - 1, 0), col))

    halo_cols = d_pool // d_attn * 3
    in_specs = [
        pl.BlockSpec((tile, d), lambda b, j: (b * tps + j, 0)),
        cur(0),
        prev(1), cur(1),
        prev(2), cur(2),
        pl.BlockSpec((POOL_HALO, d_pool),
                     lambda b, j: (jnp.maximum((b * tps + j) * hpt - 1, 0), halo_cols)),
        cur(3),
        _const_spec(bias.shape),
        _const_spec(pband.shape),
        _const_spec(phalo.shape),
        _const_spec(w_pool.shape),
        _const_spec(pool_scale.shape),
        _const_spec(w_out.shape),
    ]
    scratch = [
        ((2 * tile, d_attn), BF16),
        ((2 * tile, d_attn), BF16),
        ((tile, d_attn + d_pool), BF16),
    ]
    windows = (2 * 2 * _nbytes((tile, d), F32) + 2 * 6 * _nbytes((tile, d_attn), BF16)
               + _nbytes(bias.shape, F32) + _nbytes(pband.shape, BF16)
               + _nbytes(w_pool.shape, BF16) + _nbytes(w_out.shape, BF16)
               + sum(_nbytes(s, dt) for s, dt in scratch))
    return pl.pallas_call(
        _mix_body,
        grid=(batch, tps),
        in_specs=in_specs,
        out_specs=pl.BlockSpec((tile, d), lambda b, j: (b * tps + j, 0)),
        out_shape=jax.ShapeDtypeStruct((t, d), F32),
        scratch_shapes=[pltpu.VMEM(s, dt) for s, dt in scratch],
        compiler_params=pltpu.CompilerParams(
            dimension_semantics=("parallel", "arbitrary"),
            vmem_limit_bytes=_vmem_limit(windows)),
        name="mix",
    )(h, z, z, z, z, z, z, z, bias, pband, phalo, w_pool, pool_scale, w_out)


def _bias_body(base_ref, o_ref):
    heads = base_ref.shape[0]
    width = base_ref.shape[2]
    qi = lax.broadcasted_iota(jnp.int32, (Q_BLOCK, KEY_BAND), 0)
    kb = lax.broadcasted_iota(jnp.int32, (Q_BLOCK, KEY_BAND), 1)
    q_chunk_start = qi & ~(CHUNK - 1)
    valid = (kb >= q_chunk_start) & (kb < q_chunk_start + LEFT + CHUNK)
    for h in range(heads):
        row = jnp.broadcast_to(base_ref[h], (Q_BLOCK, width))
        toep = pltpu.roll(row, 0, 1, stride=1, stride_axis=0)
        o_ref[h // 2, (h % 2) * Q_BLOCK:(h % 2 + 1) * Q_BLOCK, :] = jnp.where(
            valid, toep[:, 0:KEY_BAND], NEG_INF)


def _attn_bias(rel_table):
    heads = rel_table.shape[0]
    assert KEY_BAND - 1 - LEFT <= REL_CLIP
    far = rel_table[:, 2 * REL_CLIP:]
    base = jnp.concatenate([
        jnp.broadcast_to(far, (heads, LEFT - REL_CLIP + 1)),
        rel_table[:, 2 * REL_CLIP - 1:REL_CLIP + LEFT - KEY_BAND:-1],
        jnp.broadcast_to(far, (heads, Q_BLOCK)),
    ], axis=1) * LOG2_E
    width = KEY_BAND + Q_BLOCK
    assert base.shape == (heads, width)
    return pl.pallas_call(
        _bias_body,
        out_shape=jax.ShapeDtypeStruct((heads // 2, 2 * Q_BLOCK, KEY_BAND), F32),
        name="attn_bias",
    )(base.reshape(heads, 1, width))


def _cross_body(h_ref, wq_ref, k_ref, v_ref, wo_ref, o_ref, oc_ref):
    tile = h_ref.shape[0]
    half = tile // 2
    xn = [_rms(h_ref[r:r + half, :]).astype(BF16) for r in (0, half)]
    q = jnp.concatenate(
        [jnp.dot(x, wq_ref[...], preferred_element_type=F32).astype(BF16) for x in xn], axis=0)
    scale2 = CROSS_HEAD_DIM ** -0.5 * LOG2_E
    head_cols = [slice(hh * CROSS_HEAD_DIM, (hh + 1) * CROSS_HEAD_DIM)
                 for hh in range(CROSS_HEADS)]
    s_all = [lax.dot_general(q[:, c], k_ref[:, c], _NT, preferred_element_type=F32) * scale2
             for c in head_cols]
    e_all = [jnp.exp2(s - jnp.max(s, axis=-1, keepdims=True)).astype(BF16) for s in s_all]
    ones_cols = jnp.ones((k_ref.shape[0], CROSS_HEAD_DIM), BF16)
    for c, e in zip(head_cols, e_all):
        rhs = jnp.concatenate([v_ref[:, c], ones_cols], axis=1)
        ol = jnp.dot(e, rhs, preferred_element_type=F32)
        oc_ref[:, c] = (ol[:, 0:CROSS_HEAD_DIM] / ol[:, CROSS_HEAD_DIM:]).astype(BF16)
    o_ref[...] = h_ref[...] + jnp.dot(oc_ref[...], wo_ref[...], preferred_element_type=F32)


def _cross(h, w_cq, kv, w_co, *, seq, n_mem):
    t, d = h.shape
    dc = w_cq.shape[1]
    tps = seq // TOKEN_TILE
    windows = (2 * 2 * _nbytes((TOKEN_TILE, d), F32) + 2 * 2 * _nbytes((n_mem, dc), BF16)
               + 2 * _nbytes((d, dc), BF16) + _nbytes((TOKEN_TILE, dc), BF16))
    return pl.pallas_call(
        _cross_body,
        grid=(t // TOKEN_TILE,),
        in_specs=[
            pl.BlockSpec((TOKEN_TILE, d), lambda i: (i, 0)),
            _const_spec(w_cq.shape),
            pl.BlockSpec((n_mem, dc), lambda i: (i // tps, 0)),
            pl.BlockSpec((n_mem, dc), lambda i: (i // tps, 1)),
            _const_spec(w_co.shape),
        ],
        out_specs=pl.BlockSpec((TOKEN_TILE, d), lambda i: (i, 0)),
        out_shape=jax.ShapeDtypeStruct((t, d), F32),
        scratch_shapes=[pltpu.VMEM((TOKEN_TILE, dc), BF16)],
        compiler_params=pltpu.CompilerParams(
            dimension_semantics=("parallel",), vmem_limit_bytes=_vmem_limit(windows)),
        name="cross",
    )(h, w_cq, kv, kv, w_co)


def _fold(gain, w, col_scale=None):
    w = gain[:, None] * w
    if col_scale is not None:
        w = w * col_scale
    return w.astype(BF16)


def kernel(x, mem, ffn1_norm, ffn1_w_gate, ffn1_w_up, ffn1_w_down, mix_norm, w_in, rel_bias,
           w_pool, pool_scale, w_out, cross_norm, mem_norm, w_cq, w_ckv, w_co, ffn2_norm,
           ffn2_w_gate, ffn2_w_up, ffn2_w_down, final_norm):
    batch, seq, d = x.shape
    n_mem = mem.shape[1]
    depth = ffn1_norm.shape[0]
    d_attn = ATTN_HEADS * ATTN_HEAD_DIM
    assert seq % TOKEN_TILE == 0 and TOKEN_TILE % Q_BLOCK == 0 and TOKEN_TILE >= LEFT

    h = x.reshape(batch * seq, d)
    mem2 = mem.reshape(batch * n_mem, d)
    fin = final_norm.reshape(1, d)
    qscale = jnp.where(jnp.arange(w_in.shape[-1]) < d_attn,
                       ATTN_HEAD_DIM ** -0.5 * LOG2_E, 1.0).astype(F32)

    for l in range(depth):
        last = l == depth - 1
        h = _ffn(h, _fold(ffn1_norm[l], ffn1_w_gate[l]), _fold(ffn1_norm[l], ffn1_w_up[l]),
                 (FFN_RES_WEIGHT * ffn1_w_down[l]).astype(BF16), fin, final_norm=False)
        z = _proj(h, _fold(mix_norm[l], w_in[l], qscale), n_split=4, name="proj")
        h = _mix(h, z, _attn_bias(rel_bias[l]), w_pool[l].astype(BF16),
                 pool_scale[l].reshape(1, -1), w_out[l].astype(BF16), batch=batch, seq=seq)
        kv = _proj(mem2, _fold(mem_norm[l], w_ckv[l]), n_split=1, name="memkv")
        h = _cross(h, _fold(cross_norm[l], w_cq[l]), kv, w_co[l].astype(BF16),
                   seq=seq, n_mem=n_mem)
        h = _ffn(h, _fold(ffn2_norm[l], ffn2_w_gate[l]), _fold(ffn2_norm[l], ffn2_w_up[l]),
                 (FFN_RES_WEIGHT * ffn2_w_down[l]).astype(BF16), fin, final_norm=last)
    return h.reshape(batch, seq, d)
```

```python
import functools

import jax
import jax.numpy as jnp
from jax import lax
from jax.experimental import pallas as pl
from jax.experimental.pallas import tpu as pltpu

F32 = jnp.float32
BF16 = jnp.bfloat16

EPS = 1e-6
NEG_INF = -1e30
FFN_RES_WEIGHT = 0.5
LOG2_E = 1.4426950408889634

CHUNK = 64
LEFT_CHUNKS = 8
REL_CLIP = 128
ATTN_HEADS = 16
ATTN_HEAD_DIM = 64
POOL_WINDOWS = (2, 4, 8, 16)
CROSS_HEADS = 4
CROSS_HEAD_DIM = 128

V7X_LANES = 128
V7X_VMEM_BYTES = 64 * 1024 * 1024
V7X_VMEM_RESERVE = 2 * 1024 * 1024

TOKEN_TILE = 512
FFN_TOKEN_TILE = 1024
FF_TILE = 512
FF_SUBCHUNKS = 2
Q_BLOCK = 2 * CHUNK
LEFT = LEFT_CHUNKS * CHUNK
KEY_BAND = LEFT + Q_BLOCK
SCORE_LEAD = 3
POOL_HALO = 16
TEMP_VMEM_BYTES = 11 * 1024 * 1024

_NT = (((1,), (1,)), ((), ()))


def _nbytes(shape, dtype):
    n = 1
    for s in shape:
        n *= s
    return n * jnp.dtype(dtype).itemsize


def _vmem_limit(window_bytes):
    return min(window_bytes + TEMP_VMEM_BYTES, V7X_VMEM_BYTES - V7X_VMEM_RESERVE)


def _rms(x):
    ms = jnp.mean(x * x, axis=-1, keepdims=True)
    return x * lax.rsqrt(ms + EPS)


def _const_spec(shape):
    nd = len(shape)
    return pl.BlockSpec(shape, lambda *_: (0,) * nd, pipeline_mode=pl.Buffered(1))


def _ffn_body(x_ref, wg_ref, wu_ref, wd_ref, fin_ref, o_ref, xn_ref, *, final_norm):
    k = pl.program_id(1)
    sub = wg_ref.shape[1] // FF_SUBCHUNKS

    def swiglu(xn_parts, base_ref):
        xn = jnp.concatenate(xn_parts, axis=0)
        gu = []
        for c in range(FF_SUBCHUNKS):
            cols = slice(c * sub, (c + 1) * sub)
            lhs = xn_parts if c == 0 else [xn]
            gu.append(tuple(
                jnp.concatenate([jnp.dot(x, w[:, cols], preferred_element_type=F32)
                                 for x in lhs], axis=0)
                for w in (wg_ref, wu_ref)))
        acc = None
        for c, (g, u) in enumerate(gu):
            a = ((g * jax.nn.sigmoid(g)) * u).astype(BF16)
            d = jnp.dot(a, wd_ref[c * sub:(c + 1) * sub, :], preferred_element_type=F32)
            acc = d if acc is None else acc + d
        o_ref[...] = base_ref[...] + acc

    @pl.when(k == 0)
    def _():
        half = x_ref.shape[0] // 2
        parts = []
        for r in (0, half):
            xn = _rms(x_ref[r:r + half, :]).astype(BF16)
            xn_ref[r:r + half, :] = xn
            parts.append(xn)
        swiglu(parts, x_ref)

    @pl.when(k > 0)
    def _():
        swiglu([xn_ref[...]], o_ref)

    if final_norm:
        @pl.when(k == pl.num_programs(1) - 1)
        def _():
            o_ref[...] = _rms(o_ref[...]) * fin_ref[...]


def _ffn(h, wg, wu, wd_half, fin, *, final_norm):
    t, d = h.shape
    dff = wg.shape[1]
    tm = FFN_TOKEN_TILE
    windows = (2 * 2 * _nbytes((tm, d), F32) + 2 * 3 * _nbytes((d, FF_TILE), BF16)
               + _nbytes((tm, d), BF16))
    return pl.pallas_call(
        functools.partial(_ffn_body, final_norm=final_norm),
        grid=(t // tm, dff // FF_TILE),
        in_specs=[
            pl.BlockSpec((tm, d), lambda i, k: (i, 0)),
            pl.BlockSpec((d, FF_TILE), lambda i, k: (0, k)),
            pl.BlockSpec((d, FF_TILE), lambda i, k: (0, k)),
            pl.BlockSpec((FF_TILE, d), lambda i, k: (k, 0)),
            pl.BlockSpec((1, d), lambda i, k: (0, 0)),
        ],
        out_specs=pl.BlockSpec((tm, d), lambda i, k: (i, 0)),
        out_shape=jax.ShapeDtypeStruct((t, d), F32),
        scratch_shapes=[pltpu.VMEM((tm, d), BF16)],
        compiler_params=pltpu.CompilerParams(
            dimension_semantics=("parallel", "arbitrary"),
            vmem_limit_bytes=_vmem_limit(windows)),
        name="ffn_final" if final_norm else "ffn",
    )(h, wg, wu, wd_half, fin)


def _proj_body(h_ref, w_ref, z_ref, *, n_split):
    half = h_ref.shape[0] // 2
    xn = [_rms(h_ref[r:r + half, :]).astype(BF16) for r in (0, half)]
    n = w_ref.shape[1] // n_split
    for r, x in zip((0, half), xn):
        z = jnp.dot(x, w_ref[:, 0:n], preferred_element_type=F32)
        z_ref[r:r + half, 0:n] = z.astype(z_ref.dtype)
    xn = jnp.concatenate(xn, axis=0)
    for c in range(1, n_split):
        z = jnp.dot(xn, w_ref[:, c * n:(c + 1) * n], preferred_element_type=F32)
        z_ref[:, c * n:(c + 1) * n] = z.astype(z_ref.dtype)


def _proj(h, w, *, n_split, name):
    t, d = h.shape
    n = w.shape[1]
    windows = (2 * _nbytes((TOKEN_TILE, d), F32) + 2 * _nbytes((TOKEN_TILE, n), BF16)
               + _nbytes((d, n), BF16))
    return pl.pallas_call(
        functools.partial(_proj_body, n_split=n_split),
        grid=(t // TOKEN_TILE,),
        in_specs=[
            pl.BlockSpec((TOKEN_TILE, d), lambda i: (i, 0)),
            _const_spec((d, n)),
        ],
        out_specs=pl.BlockSpec((TOKEN_TILE, n), lambda i: (i, 0)),
        out_shape=jax.ShapeDtypeStruct((t, n), BF16),
        compiler_params=pltpu.CompilerParams(
            dimension_semantics=("parallel",), vmem_limit_bytes=_vmem_limit(windows)),
        name=name,
    )(h, w)


def _mix_body(h_ref, q_ref, kp_ref, kc_ref, vp_ref, vc_ref, uh_ref, uc_ref, bias_ref,
              pb_ref, ph_ref, wp_ref, ps_ref, wo_ref, o_ref, y_ref):
    j = pl.program_id(1)
    tile = q_ref.shape[0]
    d_attn = q_ref.shape[1]
    n_pairs = d_attn // V7X_LANES

    lane = lax.broadcasted_iota(jnp.int32, (Q_BLOCK, V7X_LANES), 1)
    low = lane < ATTN_HEAD_DIM
    ones_cols = jnp.ones((KEY_BAND, V7X_LANES), BF16)
    kbi = lax.broadcasted_iota(jnp.int32, (1, KEY_BAND), 1)

    def band(prev_ref, cur_ref, i, cols):
        b0 = tile - LEFT + i * Q_BLOCK
        parts = [prev_ref[b0:tile, cols]] if b0 < tile else []
        parts.append(cur_ref[max(b0 - tile, 0):b0 + KEY_BAND - tile, cols])
        return jnp.concatenate(parts, axis=0)

    def biased_scores(i, p):
        cols = slice(p * V7X_LANES, (p + 1) * V7X_LANES)
        qb = q_ref[i * Q_BLOCK:(i + 1) * Q_BLOCK, cols]
        zero = jnp.zeros_like(qb)
        qbd = jnp.concatenate([jnp.where(low, qb, zero), jnp.where(low, zero, qb)], axis=0)
        s = lax.dot_general(qbd, band(kp_ref, kc_ref, i, cols), _NT, preferred_element_type=F32)
        kpos = j * tile + (i * Q_BLOCK - LEFT) + kbi
        return s + (bias_ref[p] + jnp.where(kpos >= 0, 0.0, NEG_INF).astype(F32))

    units = [(i, p) for i in range(tile // Q_BLOCK) for p in range(n_pairs)]
    s_all = [biased_scores(i, p) for i, p in units[:SCORE_LEAD]]
    for n, (i, p) in enumerate(units):
        s = s_all[n]
        s_all[n] = None
        m = jnp.max(s, axis=-1, keepdims=True)
        e = jnp.exp2(s - m).astype(BF16)
        if n + SCORE_LEAD < len(units):
            s_all.append(biased_scores(*units[n + SCORE_LEAD]))
        cols = slice(p * V7X_LANES, (p + 1) * V7X_LANES)
        rhs = jnp.concatenate([band(vp_ref, vc_ref, i, cols), ones_cols], axis=1)
        ol = jnp.dot(e, rhs, preferred_element_type=F32)
        o = ol[:, 0:V7X_LANES] / ol[:, V7X_LANES:2 * V7X_LANES]
        op = jnp.where(low, o[0:Q_BLOCK], o[Q_BLOCK:2 * Q_BLOCK])
        y_ref[i * Q_BLOCK:(i + 1) * Q_BLOCK, cols] = op.astype(BF16)

    uh = uh_ref[...]
    uh = jnp.where(j == 0, jnp.zeros_like(uh), uh)
    t_glob = lax.broadcasted_iota(jnp.int32, (tile, 1), 0) + j * tile
    gdim = uc_ref.shape[1] // len(POOL_WINDOWS)

    def window_sum(g):
        cols = slice(g * gdim, (g + 1) * gdim)
        wsum = jnp.dot(pb_ref[g], uc_ref[:, cols], preferred_element_type=F32)
        head = wsum[0:POOL_HALO] + jnp.dot(ph_ref[g], uh[:, cols], preferred_element_type=F32)
        return jnp.concatenate([head, wsum[POOL_HALO:]], axis=0)

    def delta(g, wsum):
        cols = slice(g * gdim, (g + 1) * gdim)
        cnt = jnp.minimum(t_glob + 1, POOL_WINDOWS[g]).astype(F32)
        return (wsum / cnt - uc_ref[:, cols].astype(F32)).astype(BF16)

    def pool_out(g, dlt):
        cols = slice(g * gdim, (g + 1) * gdim)
        yp = jnp.dot(dlt, wp_ref[g], preferred_element_type=F32) * ps_ref[:, cols]
        y_ref[:, d_attn + g * gdim:d_attn + (g + 1) * gdim] = yp.astype(BF16)

    def project_attn(c, n_chunks):
        n = o_ref.shape[1] // n_chunks
        cols = slice(c * n, (c + 1) * n)
        o_ref[:, cols] = h_ref[:, cols] + jnp.dot(
            y_ref[:, 0:d_attn], wo_ref[0:d_attn, cols], preferred_element_type=F32)

    n_groups = len(POOL_WINDOWS)
    wsums = [window_sum(g) for g in range(n_groups)]
    project_attn(0, n_groups)
    for g in range(n_groups):
        pool_out(g, delta(g, wsums[g]))
    for c in range(1, n_groups):
        project_attn(c, n_groups)
    d_mix = y_ref.shape[1]
    o_ref[...] += jnp.dot(y_ref[:, d_attn:d_mix], wo_ref[d_attn:d_mix, :],
                          preferred_element_type=F32)


def _pool_bands(tile):
    t = jnp.arange(tile)[:, None]
    c = jnp.arange(tile)[None, :]
    th = jnp.arange(POOL_HALO)[:, None]
    ch = jnp.arange(POOL_HALO)[None, :] - POOL_HALO
    inner = [((c <= t) & (c > t - w)) for w in POOL_WINDOWS]
    halo = [(ch > th - w) for w in POOL_WINDOWS]
    return jnp.stack(inner).astype(BF16), jnp.stack(halo).astype(BF16)


def _mix(h, z, bias, w_pool, pool_scale, w_out, *, batch, seq):
    t, d = h.shape
    d_attn = ATTN_HEADS * ATTN_HEAD_DIM
    d_pool = z.shape[1] - 3 * d_attn
    assert d_attn == d_pool and max(POOL_WINDOWS) - 1 <= POOL_HALO
    tile = TOKEN_TILE
    tps = seq // tile
    hpt = tile // POOL_HALO
    pband, phalo = _pool_bands(tile)

    def cur(col):
        return pl.BlockSpec((tile, d_attn), lambda b, j: (b * tps + j, col))

    def prev(col):
        return pl.BlockSpec((tile, d_attn), lambda b, j: (jnp.maximum(b * tps + j - 1, 0), col))

    halo_cols = d_pool // d_attn * 3
    in_specs = [
        pl.BlockSpec((tile, d), lambda b, j: (b * tps + j, 0)),
        cur(0),
        prev(1), cur(1),
        prev(2), cur(2),
        pl.BlockSpec((POOL_HALO, d_pool),
                     lambda b, j: (jnp.maximum((b * tps + j) * hpt - 1, 0), halo_cols)),
        cur(3),
        _const_spec(bias.shape),
        _const_spec(pband.shape),
        _const_spec(phalo.shape),
        _const_spec(w_pool.shape),
        _const_spec(pool_scale.shape),
        _const_spec(w_out.shape),
    ]
    scratch = [((tile, d_attn + d_pool), BF16)]
    windows = (2 * 2 * _nbytes((tile, d), F32) + 2 * 6 * _nbytes((tile, d_attn), BF16)
               + _nbytes(bias.shape, F32) + _nbytes(pband.shape, BF16)
               + _nbytes(w_pool.shape, BF16) + _nbytes(w_out.shape, BF16)
               + sum(_nbytes(s, dt) for s, dt in scratch))
    return pl.pallas_call(
        _mix_body,
        grid=(batch, tps),
        in_specs=in_specs,
        out_specs=pl.BlockSpec((tile, d), lambda b, j: (b * tps + j, 0)),
        out_shape=jax.ShapeDtypeStruct((t, d), F32),
        scratch_shapes=[pltpu.VMEM(s, dt) for s, dt in scratch],
        compiler_params=pltpu.CompilerParams(
            dimension_semantics=("parallel", "arbitrary"),
            vmem_limit_bytes=_vmem_limit(windows)),
        name="mix",
    )(h, z, z, z, z, z, z, z, bias, pband, phalo, w_pool, pool_scale, w_out)


def _bias_body(base_ref, o_ref):
    heads = base_ref.shape[0]
    width = base_ref.shape[2]
    qi = lax.broadcasted_iota(jnp.int32, (Q_BLOCK, KEY_BAND), 0)
    kb = lax.broadcasted_iota(jnp.int32, (Q_BLOCK, KEY_BAND), 1)
    q_chunk_start = qi & ~(CHUNK - 1)
    valid = (kb >= q_chunk_start) & (kb < q_chunk_start + LEFT + CHUNK)
    for h in range(heads):
        row = jnp.broadcast_to(base_ref[h], (Q_BLOCK, width))
        toep = pltpu.roll(row, 0, 1, stride=1, stride_axis=0)
        o_ref[h // 2, (h % 2) * Q_BLOCK:(h % 2 + 1) * Q_BLOCK, :] = jnp.where(
            valid, toep[:, 0:KEY_BAND], NEG_INF)


def _attn_bias(rel_table):
    heads = rel_table.shape[0]
    assert KEY_BAND - 1 - LEFT <= REL_CLIP
    far = rel_table[:, 2 * REL_CLIP:]
    base = jnp.concatenate([
        jnp.broadcast_to(far, (heads, LEFT - REL_CLIP + 1)),
        rel_table[:, 2 * REL_CLIP - 1:REL_CLIP + LEFT - KEY_BAND:-1],
        jnp.broadcast_to(far, (heads, Q_BLOCK)),
    ], axis=1) * LOG2_E
    width = KEY_BAND + Q_BLOCK
    assert base.shape == (heads, width)
    return pl.pallas_call(
        _bias_body,
        out_shape=jax.ShapeDtypeStruct((heads // 2, 2 * Q_BLOCK, KEY_BAND), F32),
        name="attn_bias",
    )(base.reshape(heads, 1, width))


def _cross_body(h_ref, wq_ref, k_ref, v_ref, wo_ref, o_ref, oc_ref):
    tile = h_ref.shape[0]
    half = tile // 2
    xn = [_rms(h_ref[r:r + half, :]).astype(BF16) for r in (0, half)]
    q = jnp.concatenate(
        [jnp.dot(x, wq_ref[...], preferred_element_type=F32).astype(BF16) for x in xn], axis=0)
    scale2 = CROSS_HEAD_DIM ** -0.5 * LOG2_E
    head_cols = [slice(hh * CROSS_HEAD_DIM, (hh + 1) * CROSS_HEAD_DIM)
                 for hh in range(CROSS_HEADS)]
    s_all = [lax.dot_general(q[:, c], k_ref[:, c], _NT, preferred_element_type=F32) * scale2
             for c in head_cols]
    e_all = [jnp.exp2(s - jnp.max(s, axis=-1, keepdims=True)).astype(BF16) for s in s_all]
    ones_cols = jnp.ones((k_ref.shape[0], CROSS_HEAD_DIM), BF16)
    for c, e in zip(head_cols, e_all):
        rhs = jnp.concatenate([v_ref[:, c], ones_cols], axis=1)
        ol = jnp.dot(e, rhs, preferred_element_type=F32)
        oc_ref[:, c] = (ol[:, 0:CROSS_HEAD_DIM] / ol[:, CROSS_HEAD_DIM:]).astype(BF16)
    o_ref[...] = h_ref[...] + jnp.dot(oc_ref[...], wo_ref[...], preferred_element_type=F32)


def _cross(h, w_cq, kv, w_co, *, seq, n_mem):
    t, d = h.shape
    dc = w_cq.shape[1]
    tps = seq // TOKEN_TILE
    windows = (2 * 2 * _nbytes((TOKEN_TILE, d), F32) + 2 * 2 * _nbytes((n_mem, dc), BF16)
               + 2 * _nbytes((d, dc), BF16) + _nbytes((TOKEN_TILE, dc), BF16))
    return pl.pallas_call(
        _cross_body,
        grid=(t // TOKEN_TILE,),
        in_specs=[
            pl.BlockSpec((TOKEN_TILE, d), lambda i: (i, 0)),
            _const_spec(w_cq.shape),
            pl.BlockSpec((n_mem, dc), lambda i: (i // tps, 0)),
            pl.BlockSpec((n_mem, dc), lambda i: (i // tps, 1)),
            _const_spec(w_co.shape),
        ],
        out_specs=pl.BlockSpec((TOKEN_TILE, d), lambda i: (i, 0)),
        out_shape=jax.ShapeDtypeStruct((t, d), F32),
        scratch_shapes=[pltpu.VMEM((TOKEN_TILE, dc), BF16)],
        compiler_params=pltpu.CompilerParams(
            dimension_semantics=("parallel",), vmem_limit_bytes=_vmem_limit(windows)),
        name="cross",
    )(h, w_cq, kv, kv, w_co)


def _fold(gain, w, col_scale=None):
    w = gain[:, None] * w
    if col_scale is not None:
        w = w * col_scale
    return w.astype(BF16)


def kernel(x, mem, ffn1_norm, ffn1_w_gate, ffn1_w_up, ffn1_w_down, mix_norm, w_in, rel_bias,
           w_pool, pool_scale, w_out, cross_norm, mem_norm, w_cq, w_ckv, w_co, ffn2_norm,
           ffn2_w_gate, ffn2_w_up, ffn2_w_down, final_norm):
    batch, seq, d = x.shape
    n_mem = mem.shape[1]
    depth = ffn1_norm.shape[0]
    d_attn = ATTN_HEADS * ATTN_HEAD_DIM
    assert seq % TOKEN_TILE == 0 and TOKEN_TILE % Q_BLOCK == 0 and TOKEN_TILE >= LEFT

    h = x.reshape(batch * seq, d)
    mem2 = mem.reshape(batch * n_mem, d)
    fin = final_norm.reshape(1, d)
    qscale = jnp.where(jnp.arange(w_in.shape[-1]) < d_attn,
                       ATTN_HEAD_DIM ** -0.5 * LOG2_E, 1.0).astype(F32)

    for l in range(depth):
        last = l == depth - 1
        h = _ffn(h, _fold(ffn1_norm[l], ffn1_w_gate[l]), _fold(ffn1_norm[l], ffn1_w_up[l]),
                 (FFN_RES_WEIGHT * ffn1_w_down[l]).astype(BF16), fin, final_norm=False)
        z = _proj(h, _fold(mix_norm[l], w_in[l], qscale), n_split=4, name="proj")
        h = _mix(h, z, _attn_bias(rel_bias[l]), w_pool[l].astype(BF16),
                 pool_scale[l].reshape(1, -1), w_out[l].astype(BF16), batch=batch, seq=seq)
        kv = _proj(mem2, _fold(mem_norm[l], w_ckv[l]), n_split=1, name="memkv")
        h = _cross(h, _fold(cross_norm[l], w_cq[l]), kv, w_co[l].astype(BF16),
                   seq=seq, n_mem=n_mem)
        h = _ffn(h, _fold(ffn2_norm[l], ffn2_w_gate[l]), _fold(ffn2_norm[l], ffn2_w_up[l]),
                 (FFN_RES_WEIGHT * ffn2_w_down[l]).astype(BF16), fin, final_norm=last)
    return h.reshape(batch, seq, d)
```

```python
import functools

import jax
import jax.numpy as jnp
from jax import lax
from jax.experimental import pallas as pl
from jax.experimental.pallas import tpu as pltpu

F32 = jnp.float32
BF16 = jnp.bfloat16

EPS = 1e-6
NEG_INF = -1e30
FFN_RES_WEIGHT = 0.5
LOG2_E = 1.4426950408889634

CHUNK = 64
LEFT_CHUNKS = 8
REL_CLIP = 128
ATTN_HEADS = 16
ATTN_HEAD_DIM = 64
POOL_WINDOWS = (2, 4, 8, 16)
CROSS_HEADS = 4
CROSS_HEAD_DIM = 128

V7X_LANES = 128
V7X_VMEM_BYTES = 64 * 1024 * 1024
V7X_VMEM_RESERVE = 2 * 1024 * 1024

TOKEN_TILE = 512
FFN_TOKEN_TILE = 1024
FF_TILE = 512
FF_SUBCHUNKS = 2
Q_BLOCK = 2 * CHUNK
LEFT = LEFT_CHUNKS * CHUNK
KEY_BAND = LEFT + Q_BLOCK
SCORE_LEAD = 3
POOL_HALO = 16
TEMP_VMEM_BYTES = 11 * 1024 * 1024

_NT = (((1,), (1,)), ((), ()))


def _nbytes(shape, dtype):
    n = 1
    for s in shape:
        n *= s
    return n * jnp.dtype(dtype).itemsize


def _vmem_limit(window_bytes):
    return min(window_bytes + TEMP_VMEM_BYTES, V7X_VMEM_BYTES - V7X_VMEM_RESERVE)


def _rms(x):
    ms = jnp.mean(x * x, axis=-1, keepdims=True)
    return x * lax.rsqrt(ms + EPS)


def _const_spec(shape):
    nd = len(shape)
    return pl.BlockSpec(shape, lambda *_: (0,) * nd, pipeline_mode=pl.Buffered(1))


def _ffn_body(x_ref, wg_ref, wu_ref, wd_ref, fin_ref, o_ref, xn_ref, *, final_norm):
    k = pl.program_id(1)

    @pl.when(k == 0)
    def _():
        x = x_ref[...]
        xn_ref[...] = _rms(x).astype(BF16)
        o_ref[...] = x

    xn = xn_ref[...]
    sub = wg_ref.shape[1] // FF_SUBCHUNKS
    gu = []
    for c in range(FF_SUBCHUNKS):
        cols = slice(c * sub, (c + 1) * sub)
        gu.append((jnp.dot(xn, wg_ref[:, cols], preferred_element_type=F32),
                   jnp.dot(xn, wu_ref[:, cols], preferred_element_type=F32)))
    acc = None
    for c, (g, u) in enumerate(gu):
        a = ((g * jax.nn.sigmoid(g)) * (u * FFN_RES_WEIGHT)).astype(BF16)
        wd = wd_ref[c * sub:(c + 1) * sub, :].astype(BF16)
        d = jnp.dot(a, wd, preferred_element_type=F32)
        acc = d if acc is None else acc + d
    o_ref[...] += acc

    if final_norm:
        @pl.when(k == pl.num_programs(1) - 1)
        def _():
            o_ref[...] = _rms(o_ref[...]) * fin_ref[...]


def _ffn(h, wg, wu, wd, fin, *, final_norm):
    t, d = h.shape
    dff = wg.shape[1]
    tm = FFN_TOKEN_TILE
    assert FF_TILE // FF_SUBCHUNKS >= 2 * V7X_LANES
    windows = (2 * 2 * _nbytes((tm, d), F32) + 2 * 2 * _nbytes((d, FF_TILE), BF16)
               + 2 * _nbytes((FF_TILE, d), wd.dtype) + _nbytes((tm, d), BF16))
    return pl.pallas_call(
        functools.partial(_ffn_body, final_norm=final_norm),
        grid=(t // tm, dff // FF_TILE),
        in_specs=[
            pl.BlockSpec((tm, d), lambda i, k: (i, 0)),
            pl.BlockSpec((d, FF_TILE), lambda i, k: (0, k)),
            pl.BlockSpec((d, FF_TILE), lambda i, k: (0, k)),
            pl.BlockSpec((FF_TILE, d), lambda i, k: (k, 0)),
            pl.BlockSpec((1, d), lambda i, k: (0, 0)),
        ],
        out_specs=pl.BlockSpec((tm, d), lambda i, k: (i, 0)),
        out_shape=jax.ShapeDtypeStruct((t, d), F32),
        scratch_shapes=[pltpu.VMEM((tm, d), BF16)],
        compiler_params=pltpu.CompilerParams(
            dimension_semantics=("parallel", "arbitrary"),
            vmem_limit_bytes=_vmem_limit(windows)),
        name="ffn_final" if final_norm else "ffn",
    )(h, wg, wu, wd, fin)


def _proj_body(h_ref, w_ref, z_ref, *, n_split):
    half = h_ref.shape[0] // 2
    xn = [_rms(h_ref[r:r + half, :]).astype(BF16) for r in (0, half)]
    n = w_ref.shape[1] // n_split
    for r, x in zip((0, half), xn):
        z = jnp.dot(x, w_ref[:, 0:n], preferred_element_type=F32)
        z_ref[r:r + half, 0:n] = z.astype(z_ref.dtype)
    xn = jnp.concatenate(xn, axis=0)
    for c in range(1, n_split):
        z = jnp.dot(xn, w_ref[:, c * n:(c + 1) * n], preferred_element_type=F32)
        z_ref[:, c * n:(c + 1) * n] = z.astype(z_ref.dtype)


def _proj(h, w, *, n_split, name):
    t, d = h.shape
    n = w.shape[1]
    windows = (2 * _nbytes((TOKEN_TILE, d), F32) + 2 * _nbytes((TOKEN_TILE, n), BF16)
               + _nbytes((d, n), BF16))
    return pl.pallas_call(
        functools.partial(_proj_body, n_split=n_split),
        grid=(t // TOKEN_TILE,),
        in_specs=[
            pl.BlockSpec((TOKEN_TILE, d), lambda i: (i, 0)),
            _const_spec((d, n)),
        ],
        out_specs=pl.BlockSpec((TOKEN_TILE, n), lambda i: (i, 0)),
        out_shape=jax.ShapeDtypeStruct((t, n), BF16),
        compiler_params=pltpu.CompilerParams(
            dimension_semantics=("parallel",), vmem_limit_bytes=_vmem_limit(windows)),
        name=name,
    )(h, w)


def _mix_body(h_ref, q_ref, kp_ref, kc_ref, vp_ref, vc_ref, uh_ref, uc_ref, bias_ref,
              pb_ref, ph_ref, wp_ref, ps_ref, wo_ref, o_ref, y_ref):
    j = pl.program_id(1)
    tile = q_ref.shape[0]
    d_attn = q_ref.shape[1]
    n_pairs = d_attn // V7X_LANES

    lane = lax.broadcasted_iota(jnp.int32, (Q_BLOCK, V7X_LANES), 1)
    low = lane < ATTN_HEAD_DIM
    ones_cols = jnp.ones((KEY_BAND, V7X_LANES), BF16)
    kbi = lax.broadcasted_iota(jnp.int32, (1, KEY_BAND), 1)

    def band(prev_ref, cur_ref, i, cols):
        b0 = tile - LEFT + i * Q_BLOCK
        parts = [prev_ref[b0:tile, cols]] if b0 < tile else []
        parts.append(cur_ref[max(b0 - tile, 0):b0 + KEY_BAND - tile, cols])
        return jnp.concatenate(parts, axis=0)

    def biased_scores(i, p):
        cols = slice(p * V7X_LANES, (p + 1) * V7X_LANES)
        qb = q_ref[i * Q_BLOCK:(i + 1) * Q_BLOCK, cols]
        zero = jnp.zeros_like(qb)
        qbd = jnp.concatenate([jnp.where(low, qb, zero), jnp.where(low, zero, qb)], axis=0)
        s = lax.dot_general(qbd, band(kp_ref, kc_ref, i, cols), _NT, preferred_element_type=F32)
        kpos = j * tile + (i * Q_BLOCK - LEFT) + kbi
        return s + (bias_ref[p] + jnp.where(kpos >= 0, 0.0, NEG_INF).astype(F32))

    units = [(i, p) for i in range(tile // Q_BLOCK) for p in range(n_pairs)]
    s_all = [biased_scores(i, p) for i, p in units[:SCORE_LEAD]]
    for n, (i, p) in enumerate(units):
        s = s_all[n]
        s_all[n] = None
        m = jnp.max(s, axis=-1, keepdims=True)
        e = jnp.exp2(s - m).astype(BF16)
        if n + SCORE_LEAD < len(units):
            s_all.append(biased_scores(*units[n + SCORE_LEAD]))
        cols = slice(p * V7X_LANES, (p + 1) * V7X_LANES)
        rhs = jnp.concatenate([band(vp_ref, vc_ref, i, cols), ones_cols], axis=1)
        ol = jnp.dot(e, rhs, preferred_element_type=F32)
        o = ol[:, 0:V7X_LANES] / ol[:, V7X_LANES:2 * V7X_LANES]
        op = jnp.where(low, o[0:Q_BLOCK], o[Q_BLOCK:2 * Q_BLOCK])
        y_ref[i * Q_BLOCK:(i + 1) * Q_BLOCK, cols] = op.astype(BF16)

    uh = uh_ref[...]
    uh = jnp.where(j == 0, jnp.zeros_like(uh), uh)
    t_glob = lax.broadcasted_iota(jnp.int32, (tile, 1), 0) + j * tile
    gdim = uc_ref.shape[1] // len(POOL_WINDOWS)

    def window_sum(g):
        cols = slice(g * gdim, (g + 1) * gdim)
        wsum = jnp.dot(pb_ref[g], uc_ref[:, cols], preferred_element_type=F32)
        head = wsum[0:POOL_HALO] + jnp.dot(ph_ref[g], uh[:, cols], preferred_element_type=F32)
        return jnp.concatenate([head, wsum[POOL_HALO:]], axis=0)

    def delta(g, wsum):
        cols = slice(g * gdim, (g + 1) * gdim)
        cnt = jnp.minimum(t_glob + 1, POOL_WINDOWS[g]).astype(F32)
        return (wsum / cnt - uc_ref[:, cols].astype(F32)).astype(BF16)

    def pool_out(g, dlt):
        cols = slice(g * gdim, (g + 1) * gdim)
        yp = jnp.dot(dlt, wp_ref[g], preferred_element_type=F32) * ps_ref[:, cols]
        y_ref[:, d_attn + g * gdim:d_attn + (g + 1) * gdim] = yp.astype(BF16)

    def project_attn(c, n_chunks):
        n = o_ref.shape[1] // n_chunks
        cols = slice(c * n, (c + 1) * n)
        o_ref[:, cols] = h_ref[:, cols] + jnp.dot(
            y_ref[:, 0:d_attn], wo_ref[0:d_attn, cols], preferred_element_type=F32)

    n_groups = len(POOL_WINDOWS)
    wsums = [window_sum(g) for g in range(n_groups)]
    project_attn(0, n_groups)
    for g in range(n_groups):
        pool_out(g, delta(g, wsums[g]))
    for c in range(1, n_groups):
        project_attn(c, n_groups)
    d_mix = y_ref.shape[1]
    o_ref[...] += jnp.dot(y_ref[:, d_attn:d_mix], wo_ref[d_attn:d_mix, :],
                          preferred_element_type=F32)


def _pool_bands(tile):
    t = jnp.arange(tile)[:, None]
    c = jnp.arange(tile)[None, :]
    th = jnp.arange(POOL_HALO)[:, None]
    ch = jnp.arange(POOL_HALO)[None, :] - POOL_HALO
    inner = [((c <= t) & (c > t - w)) for w in POOL_WINDOWS]
    halo = [(ch > th - w) for w in POOL_WINDOWS]
    return jnp.stack(inner).astype(BF16), jnp.stack(halo).astype(BF16)


def _mix(h, z, bias, w_pool, pool_scale, w_out, *, batch, seq):
    t, d = h.shape
    d_attn = ATTN_HEADS * ATTN_HEAD_DIM
    d_pool = z.shape[1] - 3 * d_attn
    assert d_attn == d_pool and max(POOL_WINDOWS) - 1 <= POOL_HALO
    tile = TOKEN_TILE
    tps = seq // tile
    hpt = tile // POOL_HALO
    pband, phalo = _pool_bands(tile)

    def cur(col):
        return pl.BlockSpec((tile, d_attn), lambda b, j: (b * tps + j, col))

    def prev(col):
        return pl.BlockSpec((tile, d_attn), lambda b, j: (jnp.maximum(b * tps + j - 1, 0), col))

    halo_cols = d_pool // d_attn * 3
    in_specs = [
        pl.BlockSpec((tile, d), lambda b, j: (b * tps + j, 0)),
        cur(0),
        prev(1), cur(1),
        prev(2), cur(2),
        pl.BlockSpec((POOL_HALO, d_pool),
                     lambda b, j: (jnp.maximum((b * tps + j) * hpt - 1, 0), halo_cols)),
        cur(3),
        _const_spec(bias.shape),
        _const_spec(pband.shape),
        _const_spec(phalo.shape),
        _const_spec(w_pool.shape),
        _const_spec(pool_scale.shape),
        _const_spec(w_out.shape),
    ]
    scratch = [((tile, d_attn + d_pool), BF16)]
    windows = (2 * 2 * _nbytes((tile, d), F32) + 2 * 6 * _nbytes((tile, d_attn), BF16)
               + _nbytes(bias.shape, F32) + _nbytes(pband.shape, BF16)
               + _nbytes(w_pool.shape, BF16) + _nbytes(w_out.shape, BF16)
               + sum(_nbytes(s, dt) for s, dt in scratch))
    return pl.pallas_call(
        _mix_body,
        grid=(batch, tps),
        in_specs=in_specs,
        out_specs=pl.BlockSpec((tile, d), lambda b, j: (b * tps + j, 0)),
        out_shape=jax.ShapeDtypeStruct((t, d), F32),
        scratch_shapes=[pltpu.VMEM(s, dt) for s, dt in scratch],
        compiler_params=pltpu.CompilerParams(
            dimension_semantics=("parallel", "arbitrary"),
            vmem_limit_bytes=_vmem_limit(windows)),
        name="mix",
    )(h, z, z, z, z, z, z, z, bias, pband, phalo, w_pool, pool_scale, w_out)


def _bias_body(base_ref, o_ref):
    heads = base_ref.shape[0]
    width = base_ref.shape[2]
    qi = lax.broadcasted_iota(jnp.int32, (Q_BLOCK, KEY_BAND), 0)
    kb = lax.broadcasted_iota(jnp.int32, (Q_BLOCK, KEY_BAND), 1)
    q_chunk_start = qi & ~(CHUNK - 1)
    valid = (kb >= q_chunk_start) & (kb < q_chunk_start + LEFT + CHUNK)
    for h in range(heads):
        row = jnp.broadcast_to(base_ref[h], (Q_BLOCK, width))
        toep = pltpu.roll(row, 0, 1, stride=1, stride_axis=0)
        o_ref[h // 2, (h % 2) * Q_BLOCK:(h % 2 + 1) * Q_BLOCK, :] = jnp.where(
            valid, toep[:, 0:KEY_BAND], NEG_INF)


def _attn_bias(rel_table):
    heads = rel_table.shape[0]
    assert KEY_BAND - 1 - LEFT <= REL_CLIP
    far = rel_table[:, 2 * REL_CLIP:]
    base = jnp.concatenate([
        jnp.broadcast_to(far, (heads, LEFT - REL_CLIP + 1)),
        rel_table[:, 2 * REL_CLIP - 1:REL_CLIP + LEFT - KEY_BAND:-1],
        jnp.broadcast_to(far, (heads, Q_BLOCK)),
    ], axis=1) * LOG2_E
    width = KEY_BAND + Q_BLOCK
    assert base.shape == (heads, width)
    return pl.pallas_call(
        _bias_body,
        out_shape=jax.ShapeDtypeStruct((heads // 2, 2 * Q_BLOCK, KEY_BAND), F32),
        name="attn_bias",
    )(base.reshape(heads, 1, width))


def _cross_body(h_ref, wq_ref, k_ref, v_ref, wo_ref, o_ref, oc_ref):
    tile = h_ref.shape[0]
    half = tile // 2
    xn = [_rms(h_ref[r:r + half, :]).astype(BF16) for r in (0, half)]
    q = jnp.concatenate(
        [jnp.dot(x, wq_ref[...], preferred_element_type=F32).astype(BF16) for x in xn], axis=0)
    scale2 = CROSS_HEAD_DIM ** -0.5 * LOG2_E
    head_cols = [slice(hh * CROSS_HEAD_DIM, (hh + 1) * CROSS_HEAD_DIM)
                 for hh in range(CROSS_HEADS)]
    s_all = [lax.dot_general(q[:, c], k_ref[:, c], _NT, preferred_element_type=F32) * scale2
             for c in head_cols]
    e_all = [jnp.exp2(s - jnp.max(s, axis=-1, keepdims=True)).astype(BF16) for s in s_all]
    ones_cols = jnp.ones((k_ref.shape[0], CROSS_HEAD_DIM), BF16)
    for c, e in zip(head_cols, e_all):
        rhs = jnp.concatenate([v_ref[:, c], ones_cols], axis=1)
        ol = jnp.dot(e, rhs, preferred_element_type=F32)
        oc_ref[:, c] = (ol[:, 0:CROSS_HEAD_DIM] / ol[:, CROSS_HEAD_DIM:]).astype(BF16)
    o_ref[...] = h_ref[...] + jnp.dot(oc_ref[...], wo_ref[...], preferred_element_type=F32)


def _cross(h, w_cq, kv, w_co, *, seq, n_mem):
    t, d = h.shape
    dc = w_cq.shape[1]
    tps = seq // TOKEN_TILE
    windows = (2 * 2 * _nbytes((TOKEN_TILE, d), F32) + 2 * 2 * _nbytes((n_mem, dc), BF16)
               + 2 * _nbytes((d, dc), BF16) + _nbytes((TOKEN_TILE, dc), BF16))
    return pl.pallas_call(
        _cross_body,
        grid=(t // TOKEN_TILE,),
        in_specs=[
            pl.BlockSpec((TOKEN_TILE, d), lambda i: (i, 0)),
            _const_spec(w_cq.shape),
            pl.BlockSpec((n_mem, dc), lambda i: (i // tps, 0)),
            pl.BlockSpec((n_mem, dc), lambda i: (i // tps, 1)),
            _const_spec(w_co.shape),
        ],
        out_specs=pl.BlockSpec((TOKEN_TILE, d), lambda i: (i, 0)),
        out_shape=jax.ShapeDtypeStruct((t, d), F32),
        scratch_shapes=[pltpu.VMEM((TOKEN_TILE, dc), BF16)],
        compiler_params=pltpu.CompilerParams(
            dimension_semantics=("parallel",), vmem_limit_bytes=_vmem_limit(windows)),
        name="cross",
    )(h, w_cq, kv, kv, w_co)


def _fold(gain, w, col_scale=None):
    w = gain[:, None] * w
    if col_scale is not None:
        w = w * col_scale
    return w.astype(BF16)


def kernel(x, mem, ffn1_norm, ffn1_w_gate, ffn1_w_up, ffn1_w_down, mix_norm, w_in, rel_bias,
           w_pool, pool_scale, w_out, cross_norm, mem_norm, w_cq, w_ckv, w_co, ffn2_norm,
           ffn2_w_gate, ffn2_w_up, ffn2_w_down, final_norm):
    batch, seq, d = x.shape
    n_mem = mem.shape[1]
    depth = ffn1_norm.shape[0]
    d_attn = ATTN_HEADS * ATTN_HEAD_DIM
    assert seq % TOKEN_TILE == 0 and TOKEN_TILE % Q_BLOCK == 0 and TOKEN_TILE >= LEFT

    h = x.reshape(batch * seq, d)
    mem2 = mem.reshape(batch * n_mem, d)
    fin = final_norm.reshape(1, d)
    qscale = jnp.where(jnp.arange(w_in.shape[-1]) < d_attn,
                       ATTN_HEAD_DIM ** -0.5 * LOG2_E, 1.0).astype(F32)

    for l in range(depth):
        last = l == depth - 1
        h = _ffn(h, _fold(ffn1_norm[l], ffn1_w_gate[l]), _fold(ffn1_norm[l], ffn1_w_up[l]),
                 ffn1_w_down[l], fin, final_norm=False)
        z = _proj(h, _fold(mix_norm[l], w_in[l], qscale), n_split=4, name="proj")
        h = _mix(h, z, _attn_bias(rel_bias[l]), w_pool[l].astype(BF16),
                 pool_scale[l].reshape(1, -1), w_out[l].astype(BF16), batch=batch, seq=seq)
        kv = _proj(mem2, _fold(mem_norm[l], w_ckv[l]), n_split=1, name="memkv")
        h = _cross(h, _fold(cross_norm[l], w_cq[l]), kv, w_co[l].astype(BF16),
                   seq=seq, n_mem=n_mem)
        h = _ffn(h, _fold(ffn2_norm[l], ffn2_w_gate[l]), _fold(ffn2_norm[l], ffn2_w_up[l]),
                 ffn2_w_down[l], fin, final_norm=last)
    return h.reshape(batch, seq, d)
```

```python
import functools

import jax
import jax.numpy as jnp
from jax import lax
from jax.experimental import pallas as pl
from jax.experimental.pallas import tpu as pltpu

F32 = jnp.float32
BF16 = jnp.bfloat16

EPS = 1e-6
NEG_INF = -1e30
FFN_RES_WEIGHT = 0.5
LOG2_E = 1.4426950408889634

CHUNK = 64
LEFT_CHUNKS = 8
REL_CLIP = 128
ATTN_HEADS = 16
ATTN_HEAD_DIM = 64
POOL_WINDOWS = (2, 4, 8, 16)
CROSS_HEADS = 4
CROSS_HEAD_DIM = 128

V7X_LANES = 128
V7X_VMEM_BYTES = 64 * 1024 * 1024
V7X_VMEM_RESERVE = 2 * 1024 * 1024

TOKEN_TILE = 512
FFN_TOKEN_TILE = 1024
FF_TILE = 512
FF_SUBCHUNKS = 2
Q_BLOCK = 2 * CHUNK
LEFT = LEFT_CHUNKS * CHUNK
KEY_BAND = LEFT + Q_BLOCK
SCORE_LEAD = 3
POOL_HALO = 16
TEMP_VMEM_BYTES = 11 * 1024 * 1024

_NT = (((1,), (1,)), ((), ()))


def _nbytes(shape, dtype):
    n = 1
    for s in shape:
        n *= s
    return n * jnp.dtype(dtype).itemsize


def _vmem_limit(window_bytes):
    return min(window_bytes + TEMP_VMEM_BYTES, V7X_VMEM_BYTES - V7X_VMEM_RESERVE)


def _rms(x):
    ms = jnp.mean(x * x, axis=-1, keepdims=True)
    return x * lax.rsqrt(ms + EPS)


def _const_spec(shape):
    nd = len(shape)
    return pl.BlockSpec(shape, lambda *_: (0,) * nd, pipeline_mode=pl.Buffered(1))


def _ffn_body(x_ref, wg_ref, wu_ref, wd_ref, fin_ref, o_ref, xn_ref, *, final_norm):
    k = pl.program_id(1)

    @pl.when(k == 0)
    def _():
        x = x_ref[...]
        xn_ref[...] = _rms(x).astype(BF16)
        o_ref[...] = x

    sub = wg_ref.shape[1] // FF_SUBCHUNKS

    def hidden():
        xn = xn_ref[...]
        gu = []
        for c in range(FF_SUBCHUNKS):
            cols = slice(c * sub, (c + 1) * sub)
            gu.append((jnp.dot(xn, wg_ref[:, cols], preferred_element_type=F32),
                       jnp.dot(xn, wu_ref[:, cols], preferred_element_type=F32)))
        return [((g * jax.nn.sigmoid(g)) * (u * FFN_RES_WEIGHT)).astype(BF16) for g, u in gu]

    def down(acts, rows):
        acc = None
        for c, a in enumerate(acts):
            wd = wd_ref[c * sub:(c + 1) * sub, :].astype(BF16)
            d = jnp.dot(a[rows, :], wd, preferred_element_type=F32)
            acc = d if acc is None else acc + d
        return acc

    def accumulate():
        o_ref[...] += down(hidden(), slice(None))

    if not final_norm:
        accumulate()
    else:
        last = pl.num_programs(1) - 1
        pl.when(k < last)(accumulate)

        @pl.when(k == last)
        def _():
            acts = hidden()
            half = o_ref.shape[0] // 2
            for r in (0, half):
                rows = slice(r, r + half)
                o_ref[rows, :] = _rms(o_ref[rows, :] + down(acts, rows)) * fin_ref[...]


def _ffn(h, wg, wu, wd, fin, *, final_norm):
    t, d = h.shape
    dff = wg.shape[1]
    tm = FFN_TOKEN_TILE
    assert FF_TILE // FF_SUBCHUNKS >= 2 * V7X_LANES
    windows = (2 * 2 * _nbytes((tm, d), F32) + 2 * 2 * _nbytes((d, FF_TILE), BF16)
               + 2 * _nbytes((FF_TILE, d), wd.dtype) + _nbytes((tm, d), BF16))
    return pl.pallas_call(
        functools.partial(_ffn_body, final_norm=final_norm),
        grid=(t // tm, dff // FF_TILE),
        in_specs=[
            pl.BlockSpec((tm, d), lambda i, k: (i, 0)),
            pl.BlockSpec((d, FF_TILE), lambda i, k: (0, k)),
            pl.BlockSpec((d, FF_TILE), lambda i, k: (0, k)),
            pl.BlockSpec((FF_TILE, d), lambda i, k: (k, 0)),
            pl.BlockSpec((1, d), lambda i, k: (0, 0)),
        ],
        out_specs=pl.BlockSpec((tm, d), lambda i, k: (i, 0)),
        out_shape=jax.ShapeDtypeStruct((t, d), F32),
        scratch_shapes=[pltpu.VMEM((tm, d), BF16)],
        compiler_params=pltpu.CompilerParams(
            dimension_semantics=("parallel", "arbitrary"),
            vmem_limit_bytes=_vmem_limit(windows)),
        name="ffn_final" if final_norm else "ffn",
    )(h, wg, wu, wd, fin)


def _proj_body(h_ref, g_ref, w_ref, cs_ref, z_ref, *, n_split):
    half = h_ref.shape[0] // 2
    xn = [(_rms(h_ref[r:r + half, :]) * g_ref[...]).astype(BF16) for r in (0, half)]
    n = w_ref.shape[1] // n_split
    w0 = w_ref[:, 0:n].astype(BF16)
    for r, x in zip((0, half), xn):
        z = jnp.dot(x, w0, preferred_element_type=F32) * cs_ref[:, 0:n]
        z_ref[r:r + half, 0:n] = z.astype(z_ref.dtype)
    xn = jnp.concatenate(xn, axis=0)
    for c in range(1, n_split):
        cols = slice(c * n, (c + 1) * n)
        z = jnp.dot(xn, w_ref[:, cols].astype(BF16), preferred_element_type=F32) * cs_ref[:, cols]
        z_ref[:, cols] = z.astype(z_ref.dtype)


def _proj(h, gain, w, col_scale, *, n_split, name):
    t, d = h.shape
    n = w.shape[1]
    windows = (2 * _nbytes((TOKEN_TILE, d), F32) + 2 * _nbytes((TOKEN_TILE, n), BF16)
               + _nbytes((d, n), F32))
    return pl.pallas_call(
        functools.partial(_proj_body, n_split=n_split),
        grid=(t // TOKEN_TILE,),
        in_specs=[
            pl.BlockSpec((TOKEN_TILE, d), lambda i: (i, 0)),
            _const_spec((1, d)),
            _const_spec((d, n)),
            _const_spec((1, n)),
        ],
        out_specs=pl.BlockSpec((TOKEN_TILE, n), lambda i: (i, 0)),
        out_shape=jax.ShapeDtypeStruct((t, n), BF16),
        compiler_params=pltpu.CompilerParams(
            dimension_semantics=("parallel",), vmem_limit_bytes=_vmem_limit(windows)),
        name=name,
    )(h, gain.reshape(1, d), w, col_scale.reshape(1, n))


def _mix_body(h_ref, q_ref, kp_ref, kc_ref, vp_ref, vc_ref, uh_ref, uc_ref, bias_ref,
              pb_ref, ph_ref, wp_ref, ps_ref, wo_ref, o_ref, y_ref):
    j = pl.program_id(1)
    tile = q_ref.shape[0]
    d_attn = q_ref.shape[1]
    n_pairs = d_attn // V7X_LANES

    lane = lax.broadcasted_iota(jnp.int32, (Q_BLOCK, V7X_LANES), 1)
    low = lane < ATTN_HEAD_DIM
    ones_cols = jnp.ones((KEY_BAND, V7X_LANES), BF16)
    kbi = lax.broadcasted_iota(jnp.int32, (1, KEY_BAND), 1)

    def band(prev_ref, cur_ref, i, cols):
        b0 = tile - LEFT + i * Q_BLOCK
        parts = [prev_ref[b0:tile, cols]] if b0 < tile else []
        parts.append(cur_ref[max(b0 - tile, 0):b0 + KEY_BAND - tile, cols])
        return jnp.concatenate(parts, axis=0)

    def biased_scores(i, p):
        cols = slice(p * V7X_LANES, (p + 1) * V7X_LANES)
        qb = q_ref[i * Q_BLOCK:(i + 1) * Q_BLOCK, cols]
        zero = jnp.zeros_like(qb)
        qbd = jnp.concatenate([jnp.where(low, qb, zero), jnp.where(low, zero, qb)], axis=0)
        s = lax.dot_general(qbd, band(kp_ref, kc_ref, i, cols), _NT, preferred_element_type=F32)
        kpos = j * tile + (i * Q_BLOCK - LEFT) + kbi
        return s + (bias_ref[p] + jnp.where(kpos >= 0, 0.0, NEG_INF).astype(F32))

    units = [(i, p) for i in range(tile // Q_BLOCK) for p in range(n_pairs)]
    s_all = [biased_scores(i, p) for i, p in units[:SCORE_LEAD]]
    for n, (i, p) in enumerate(units):
        s = s_all[n]
        s_all[n] = None
        m = jnp.max(s, axis=-1, keepdims=True)
        e = jnp.exp2(s - m).astype(BF16)
        if n + SCORE_LEAD < len(units):
            s_all.append(biased_scores(*units[n + SCORE_LEAD]))
        cols = slice(p * V7X_LANES, (p + 1) * V7X_LANES)
        rhs = jnp.concatenate([band(vp_ref, vc_ref, i, cols), ones_cols], axis=1)
        ol = jnp.dot(e, rhs, preferred_element_type=F32)
        o = ol[:, 0:V7X_LANES] / ol[:, V7X_LANES:2 * V7X_LANES]
        op = jnp.where(low, o[0:Q_BLOCK], o[Q_BLOCK:2 * Q_BLOCK])
        y_ref[i * Q_BLOCK:(i + 1) * Q_BLOCK, cols] = op.astype(BF16)

    uh = uh_ref[...]
    uh = jnp.where(j == 0, jnp.zeros_like(uh), uh)
    t_glob = lax.broadcasted_iota(jnp.int32, (tile, 1), 0) + j * tile
    gdim = uc_ref.shape[1] // len(POOL_WINDOWS)

    def window_sum(g):
        cols = slice(g * gdim, (g + 1) * gdim)
        wsum = jnp.dot(pb_ref[g], uc_ref[:, cols], preferred_element_type=F32)
        head = wsum[0:POOL_HALO] + jnp.dot(ph_ref[g], uh[:, cols], preferred_element_type=F32)
        return jnp.concatenate([head, wsum[POOL_HALO:]], axis=0)

    def delta(g, wsum):
        cols = slice(g * gdim, (g + 1) * gdim)
        cnt = jnp.minimum(t_glob + 1, POOL_WINDOWS[g]).astype(F32)
        return (wsum / cnt - uc_ref[:, cols].astype(F32)).astype(BF16)

    def pool_out(g, dlt):
        cols = slice(g * gdim, (g + 1) * gdim)
        yp = jnp.dot(dlt, wp_ref[g], preferred_element_type=F32) * ps_ref[:, cols]
        y_ref[:, d_attn + g * gdim:d_attn + (g + 1) * gdim] = yp.astype(BF16)

    def project_attn(c, n_chunks):
        n = o_ref.shape[1] // n_chunks
        cols = slice(c * n, (c + 1) * n)
        o_ref[:, cols] = h_ref[:, cols] + jnp.dot(
            y_ref[:, 0:d_attn], wo_ref[0:d_attn, cols], preferred_element_type=F32)

    n_groups = len(POOL_WINDOWS)
    wsums = [window_sum(g) for g in range(n_groups)]
    project_attn(0, n_groups)
    for g in range(n_groups):
        pool_out(g, delta(g, wsums[g]))
    for c in range(1, n_groups):
        project_attn(c, n_groups)
    d_mix = y_ref.shape[1]
    o_ref[...] += jnp.dot(y_ref[:, d_attn:d_mix], wo_ref[d_attn:d_mix, :],
                          preferred_element_type=F32)


def _pool_bands(tile):
    t = jnp.arange(tile)[:, None]
    c = jnp.arange(tile)[None, :]
    th = jnp.arange(POOL_HALO)[:, None]
    ch = jnp.arange(POOL_HALO)[None, :] - POOL_HALO
    inner = [((c <= t) & (c > t - w)) for w in POOL_WINDOWS]
    halo = [(ch > th - w) for w in POOL_WINDOWS]
    return jnp.stack(inner).astype(BF16), jnp.stack(halo).astype(BF16)


def _mix(h, z, bias, w_pool, pool_scale, w_out, *, batch, seq):
    t, d = h.shape
    d_attn = ATTN_HEADS * ATTN_HEAD_DIM
    d_pool = z.shape[1] - 3 * d_attn
    assert d_attn == d_pool and max(POOL_WINDOWS) - 1 <= POOL_HALO
    tile = TOKEN_TILE
    tps = seq // tile
    hpt = tile // POOL_HALO
    pband, phalo = _pool_bands(tile)

    def cur(col):
        return pl.BlockSpec((tile, d_attn), lambda b, j: (b * tps + j, col))

    def prev(col):
        return pl.BlockSpec((tile, d_attn), lambda b, j: (jnp.maximum(b * tps + j - 1, 0), col))

    halo_cols = d_pool // d_attn * 3
    in_specs = [
        pl.BlockSpec((tile, d), lambda b, j: (b * tps + j, 0)),
        cur(0),
        prev(1), cur(1),
        prev(2), cur(2),
        pl.BlockSpec((POOL_HALO, d_pool),
                     lambda b, j: (jnp.maximum((b * tps + j) * hpt - 1, 0), halo_cols)),
        cur(3),
        _const_spec(bias.shape),
        _const_spec(pband.shape),
        _const_spec(phalo.shape),
        _const_spec(w_pool.shape),
        _const_spec(pool_scale.shape),
        _const_spec(w_out.shape),
    ]
    scratch = [((tile, d_attn + d_pool), BF16)]
    windows = (2 * 2 * _nbytes((tile, d), F32) + 2 * 6 * _nbytes((tile, d_attn), BF16)
               + _nbytes(bias.shape, F32) + _nbytes(pband.shape, BF16)
               + _nbytes(w_pool.shape, BF16) + _nbytes(w_out.shape, BF16)
               + sum(_nbytes(s, dt) for s, dt in scratch))
    return pl.pallas_call(
        _mix_body,
        grid=(batch, tps),
        in_specs=in_specs,
        out_specs=pl.BlockSpec((tile, d), lambda b, j: (b * tps + j, 0)),
        out_shape=jax.ShapeDtypeStruct((t, d), F32),
        scratch_shapes=[pltpu.VMEM(s, dt) for s, dt in scratch],
        compiler_params=pltpu.CompilerParams(
            dimension_semantics=("parallel", "arbitrary"),
            vmem_limit_bytes=_vmem_limit(windows)),
        name="mix",
    )(h, z, z, z, z, z, z, z, bias, pband, phalo, w_pool, pool_scale, w_out)


def _bias_body(base_ref, o_ref):
    heads = base_ref.shape[0]
    width = base_ref.shape[2]
    qi = lax.broadcasted_iota(jnp.int32, (Q_BLOCK, KEY_BAND), 0)
    kb = lax.broadcasted_iota(jnp.int32, (Q_BLOCK, KEY_BAND), 1)
    q_chunk_start = qi & ~(CHUNK - 1)
    valid = (kb >= q_chunk_start) & (kb < q_chunk_start + LEFT + CHUNK)
    for h in range(heads):
        row = jnp.broadcast_to(base_ref[h], (Q_BLOCK, width))
        toep = pltpu.roll(row, 0, 1, stride=1, stride_axis=0)
        o_ref[h // 2, (h % 2) * Q_BLOCK:(h % 2 + 1) * Q_BLOCK, :] = jnp.where(
            valid, toep[:, 0:KEY_BAND], NEG_INF)


def _attn_bias(rel_table):
    heads = rel_table.shape[0]
    assert KEY_BAND - 1 - LEFT <= REL_CLIP
    far = rel_table[:, 2 * REL_CLIP:]
    base = jnp.concatenate([
        jnp.broadcast_to(far, (heads, LEFT - REL_CLIP + 1)),
        rel_table[:, 2 * REL_CLIP - 1:REL_CLIP + LEFT - KEY_BAND:-1],
        jnp.broadcast_to(far, (heads, Q_BLOCK)),
    ], axis=1) * LOG2_E
    width = KEY_BAND + Q_BLOCK
    assert base.shape == (heads, width)
    return pl.pallas_call(
        _bias_body,
        out_shape=jax.ShapeDtypeStruct((heads // 2, 2 * Q_BLOCK, KEY_BAND), F32),
        name="attn_bias",
    )(base.reshape(heads, 1, width))


def _cross_body(h_ref, wq_ref, k_ref, v_ref, wo_ref, o_ref, oc_ref):
    tile = h_ref.shape[0]
    half = tile // 2
    xn = [_rms(h_ref[r:r + half, :]).astype(BF16) for r in (0, half)]
    q = jnp.concatenate(
        [jnp.dot(x, wq_ref[...], preferred_element_type=F32).astype(BF16) for x in xn], axis=0)
    scale2 = CROSS_HEAD_DIM ** -0.5 * LOG2_E
    head_cols = [slice(hh * CROSS_HEAD_DIM, (hh + 1) * CROSS_HEAD_DIM)
                 for hh in range(CROSS_HEADS)]
    s_all = [lax.dot_general(q[:, c], k_ref[:, c], _NT, preferred_element_type=F32) * scale2
             for c in head_cols]
    e_all = [jnp.exp2(s - jnp.max(s, axis=-1, keepdims=True)).astype(BF16) for s in s_all]
    ones_cols = jnp.ones((k_ref.shape[0], CROSS_HEAD_DIM), BF16)
    for c, e in zip(head_cols, e_all):
        rhs = jnp.concatenate([v_ref[:, c], ones_cols], axis=1)
        ol = jnp.dot(e, rhs, preferred_element_type=F32)
        oc_ref[:, c] = (ol[:, 0:CROSS_HEAD_DIM] / ol[:, CROSS_HEAD_DIM:]).astype(BF16)
    o_ref[...] = h_ref[...] + jnp.dot(oc_ref[...], wo_ref[...], preferred_element_type=F32)


def _cross(h, w_cq, kv, w_co, *, seq, n_mem):
    t, d = h.shape
    dc = w_cq.shape[1]
    tps = seq // TOKEN_TILE
    windows = (2 * 2 * _nbytes((TOKEN_TILE, d), F32) + 2 * 2 * _nbytes((n_mem, dc), BF16)
               + 2 * _nbytes((d, dc), BF16) + _nbytes((TOKEN_TILE, dc), BF16))
    return pl.pallas_call(
        _cross_body,
        grid=(t // TOKEN_TILE,),
        in_specs=[
            pl.BlockSpec((TOKEN_TILE, d), lambda i: (i, 0)),
            _const_spec(w_cq.shape),
            pl.BlockSpec((n_mem, dc), lambda i: (i // tps, 0)),
            pl.BlockSpec((n_mem, dc), lambda i: (i // tps, 1)),
            _const_spec(w_co.shape),
        ],
        out_specs=pl.BlockSpec((TOKEN_TILE, d), lambda i: (i, 0)),
        out_shape=jax.ShapeDtypeStruct((t, d), F32),
        scratch_shapes=[pltpu.VMEM((TOKEN_TILE, dc), BF16)],
        compiler_params=pltpu.CompilerParams(
            dimension_semantics=("parallel",), vmem_limit_bytes=_vmem_limit(windows)),
        name="cross",
    )(h, w_cq, kv, kv, w_co)


def _fold(gain, w):
    return (gain[:, None] * w).astype(BF16)


def kernel(x, mem, ffn1_norm, ffn1_w_gate, ffn1_w_up, ffn1_w_down, mix_norm, w_in, rel_bias,
           w_pool, pool_scale, w_out, cross_norm, mem_norm, w_cq, w_ckv, w_co, ffn2_norm,
           ffn2_w_gate, ffn2_w_up, ffn2_w_down, final_norm):
    batch, seq, d = x.shape
    n_mem = mem.shape[1]
    depth = ffn1_norm.shape[0]
    d_attn = ATTN_HEADS * ATTN_HEAD_DIM
    assert seq % TOKEN_TILE == 0 and TOKEN_TILE % Q_BLOCK == 0 and TOKEN_TILE >= LEFT

    h = x.reshape(batch * seq, d)
    mem2 = mem.reshape(batch * n_mem, d)
    fin = final_norm.reshape(1, d)
    qscale = jnp.where(jnp.arange(w_in.shape[-1]) < d_attn,
                       ATTN_HEAD_DIM ** -0.5 * LOG2_E, 1.0).astype(F32)

    for l in range(depth):
        last = l == depth - 1
        h = _ffn(h, _fold(ffn1_norm[l], ffn1_w_gate[l]), _fold(ffn1_norm[l], ffn1_w_up[l]),
                 ffn1_w_down[l], fin, final_norm=False)
        z = _proj(h, mix_norm[l], w_in[l], qscale, n_split=4, name="proj")
        h = _mix(h, z, _attn_bias(rel_bias[l]), w_pool[l].astype(BF16),
                 pool_scale[l].reshape(1, -1), w_out[l].astype(BF16), batch=batch, seq=seq)
        kv = _proj(mem2, mem_norm[l], w_ckv[l], jnp.ones((w_ckv.shape[-1],), F32),
                   n_split=1, name="memkv")
        h = _cross(h, _fold(cross_norm[l], w_cq[l]), kv, w_co[l].astype(BF16),
                   seq=seq, n_mem=n_mem)
        h = _ffn(h, _fold(ffn2_norm[l], ffn2_w_gate[l]), _fold(ffn2_norm[l], ffn2_w_up[l]),
                 ffn2_w_down[l], fin, final_norm=last)
    return h.reshape(batch, seq, d)
```

```python
import functools

import jax
import jax.numpy as jnp
from jax import lax
from jax.experimental import pallas as pl
from jax.experimental.pallas import tpu as pltpu

F32 = jnp.float32
BF16 = jnp.bfloat16

EPS = 1e-6
NEG_INF = -1e30
FFN_RES_WEIGHT = 0.5
LOG2_E = 1.4426950408889634

CHUNK = 64
LEFT_CHUNKS = 8
REL_CLIP = 128
ATTN_HEADS = 16
ATTN_HEAD_DIM = 64
POOL_WINDOWS = (2, 4, 8, 16)
CROSS_HEADS = 4
CROSS_HEAD_DIM = 128

V7X_LANES = 128
V7X_MXU_WIDTH = 256
V7X_VMEM_BYTES = 64 * 1024 * 1024
V7X_VMEM_RESERVE = 2 * 1024 * 1024

TOKEN_TILE = 512
FFN_TOKEN_TILE = 1024
FF_TILE = 512
FF_SUBCHUNKS = 2
PROJ_SPLIT = 4
Q_BLOCK = 2 * CHUNK
LEFT = LEFT_CHUNKS * CHUNK
KEY_BAND = LEFT + Q_BLOCK
SCORE_LEAD = 3
POOL_HALO = 16
TEMP_VMEM_BYTES = 11 * 1024 * 1024

_NT = (((1,), (1,)), ((), ()))


def _nbytes(shape, dtype):
    n = 1
    for s in shape:
        n *= s
    return n * jnp.dtype(dtype).itemsize


def _vmem_limit(window_bytes):
    return min(window_bytes + TEMP_VMEM_BYTES, V7X_VMEM_BYTES - V7X_VMEM_RESERVE)


def _rms(x):
    ms = jnp.mean(x * x, axis=-1, keepdims=True)
    return x * lax.rsqrt(ms + EPS)


def _const_spec(shape):
    nd = len(shape)
    return pl.BlockSpec(shape, lambda *_: (0,) * nd, pipeline_mode=pl.Buffered(1))


def _ffn_body(x_ref, wg_ref, wu_ref, wd_ref, fin_ref, o_ref, xn_ref, *, final_norm):
    k = pl.program_id(1)

    @pl.when(k == 0)
    def _():
        x = x_ref[...]
        xn_ref[...] = _rms(x).astype(BF16)
        o_ref[...] = x

    sub = wg_ref.shape[1] // FF_SUBCHUNKS

    def hidden():
        xn = xn_ref[...]
        gu = []
        for c in range(FF_SUBCHUNKS):
            cols = slice(c * sub, (c + 1) * sub)
            gu.append((jnp.dot(xn, wg_ref[:, cols], preferred_element_type=F32),
                       jnp.dot(xn, wu_ref[:, cols], preferred_element_type=F32)))
        return [((g * jax.nn.sigmoid(g)) * (u * FFN_RES_WEIGHT)).astype(BF16) for g, u in gu]

    def down(acts, rows):
        acc = None
        for c, a in enumerate(acts):
            wd = wd_ref[c * sub:(c + 1) * sub, :].astype(BF16)
            d = jnp.dot(a[rows, :], wd, preferred_element_type=F32)
            acc = d if acc is None else acc + d
        return acc

    def accumulate():
        o_ref[...] += down(hidden(), slice(None))

    if not final_norm:
        accumulate()
    else:
        last = pl.num_programs(1) - 1
        pl.when(k < last)(accumulate)

        @pl.when(k == last)
        def _():
            acts = hidden()
            half = o_ref.shape[0] // 2
            for r in (0, half):
                rows = slice(r, r + half)
                o_ref[rows, :] = _rms(o_ref[rows, :] + down(acts, rows)) * fin_ref[...]


def _ffn(h, wg, wu, wd, fin, *, final_norm):
    t, d = h.shape
    dff = wg.shape[1]
    tm = FFN_TOKEN_TILE
    assert FF_TILE // FF_SUBCHUNKS >= V7X_MXU_WIDTH
    windows = (2 * 2 * _nbytes((tm, d), F32) + 2 * 2 * _nbytes((d, FF_TILE), BF16)
               + 2 * _nbytes((FF_TILE, d), wd.dtype) + _nbytes((tm, d), BF16))
    return pl.pallas_call(
        functools.partial(_ffn_body, final_norm=final_norm),
        grid=(t // tm, dff // FF_TILE),
        in_specs=[
            pl.BlockSpec((tm, d), lambda i, k: (i, 0)),
            pl.BlockSpec((d, FF_TILE), lambda i, k: (0, k)),
            pl.BlockSpec((d, FF_TILE), lambda i, k: (0, k)),
            pl.BlockSpec((FF_TILE, d), lambda i, k: (k, 0)),
            pl.BlockSpec((1, d), lambda i, k: (0, 0)),
        ],
        out_specs=pl.BlockSpec((tm, d), lambda i, k: (i, 0)),
        out_shape=jax.ShapeDtypeStruct((t, d), F32),
        scratch_shapes=[pltpu.VMEM((tm, d), BF16)],
        compiler_params=pltpu.CompilerParams(
            dimension_semantics=("parallel", "arbitrary"),
            vmem_limit_bytes=_vmem_limit(windows)),
        name="ffn_final" if final_norm else "ffn",
    )(h, wg, wu, wd, fin)


def _proj_body(h_ref, g_ref, w_ref, cs_ref, z_ref, *, n_split):
    half = h_ref.shape[0] // 2
    xn = [(_rms(h_ref[r:r + half, :]) * g_ref[...]).astype(BF16) for r in (0, half)]
    n = w_ref.shape[1] // n_split
    w0 = w_ref[:, 0:n].astype(BF16)
    for r, x in zip((0, half), xn):
        z = jnp.dot(x, w0, preferred_element_type=F32) * cs_ref[:, 0:n]
        z_ref[r:r + half, 0:n] = z.astype(z_ref.dtype)
    xn = jnp.concatenate(xn, axis=0)
    for c in range(1, n_split):
        cols = slice(c * n, (c + 1) * n)
        z = jnp.dot(xn, w_ref[:, cols].astype(BF16), preferred_element_type=F32) * cs_ref[:, cols]
        z_ref[:, cols] = z.astype(z_ref.dtype)


def _proj(h, gain, w, col_scale, *, n_split, name):
    t, d = h.shape
    n = w.shape[1]
    windows = (2 * _nbytes((TOKEN_TILE, d), F32) + 2 * _nbytes((TOKEN_TILE, n), BF16)
               + _nbytes((d, n), F32))
    return pl.pallas_call(
        functools.partial(_proj_body, n_split=n_split),
        grid=(t // TOKEN_TILE,),
        in_specs=[
            pl.BlockSpec((TOKEN_TILE, d), lambda i: (i, 0)),
            _const_spec((1, d)),
            _const_spec((d, n)),
            _const_spec((1, n)),
        ],
        out_specs=pl.BlockSpec((TOKEN_TILE, n), lambda i: (i, 0)),
        out_shape=jax.ShapeDtypeStruct((t, n), BF16),
        compiler_params=pltpu.CompilerParams(
            dimension_semantics=("parallel",), vmem_limit_bytes=_vmem_limit(windows)),
        name=name,
    )(h, gain.reshape(1, d), w, col_scale.reshape(1, n))


def _mix_body(h_ref, q_ref, kp_ref, kc_ref, vp_ref, vc_ref, uh_ref, uc_ref, bias_ref,
              pb_ref, ph_ref, wp_ref, ps_ref, wo_ref, o_ref, y_ref):
    j = pl.program_id(1)
    tile = q_ref.shape[0]
    d_attn = q_ref.shape[1]
    n_pairs = d_attn // V7X_LANES

    lane = lax.broadcasted_iota(jnp.int32, (Q_BLOCK, V7X_LANES), 1)
    low = lane < ATTN_HEAD_DIM
    ones_cols = jnp.ones((KEY_BAND, V7X_LANES), BF16)
    kbi = lax.broadcasted_iota(jnp.int32, (1, KEY_BAND), 1)

    def band(prev_ref, cur_ref, i, cols):
        b0 = tile - LEFT + i * Q_BLOCK
        parts = [prev_ref[b0:tile, cols]] if b0 < tile else []
        parts.append(cur_ref[max(b0 - tile, 0):b0 + KEY_BAND - tile, cols])
        return jnp.concatenate(parts, axis=0)

    def biased_scores(i, p):
        cols = slice(p * V7X_LANES, (p + 1) * V7X_LANES)
        qb = q_ref[i * Q_BLOCK:(i + 1) * Q_BLOCK, cols]
        zero = jnp.zeros_like(qb)
        qbd = jnp.concatenate([jnp.where(low, qb, zero), jnp.where(low, zero, qb)], axis=0)
        s = lax.dot_general(qbd, band(kp_ref, kc_ref, i, cols), _NT, preferred_element_type=F32)
        kpos = j * tile + (i * Q_BLOCK - LEFT) + kbi
        return s + (bias_ref[p] + jnp.where(kpos >= 0, 0.0, NEG_INF).astype(F32))

    units = [(i, p) for i in range(tile // Q_BLOCK) for p in range(n_pairs)]
    s_all = [biased_scores(i, p) for i, p in units[:SCORE_LEAD]]
    for n, (i, p) in enumerate(units):
        s = s_all[n]
        s_all[n] = None
        m = jnp.max(s, axis=-1, keepdims=True)
        e = jnp.exp2(s - m).astype(BF16)
        if n + SCORE_LEAD < len(units):
            s_all.append(biased_scores(*units[n + SCORE_LEAD]))
        cols = slice(p * V7X_LANES, (p + 1) * V7X_LANES)
        rhs = jnp.concatenate([band(vp_ref, vc_ref, i, cols), ones_cols], axis=1)
        ol = jnp.dot(e, rhs, preferred_element_type=F32)
        o = ol[:, 0:V7X_LANES] / ol[:, V7X_LANES:2 * V7X_LANES]
        op = jnp.where(low, o[0:Q_BLOCK], o[Q_BLOCK:2 * Q_BLOCK])
        y_ref[i * Q_BLOCK:(i + 1) * Q_BLOCK, cols] = op.astype(BF16)

    uh = uh_ref[...]
    uh = jnp.where(j == 0, jnp.zeros_like(uh), uh)
    t_glob = lax.broadcasted_iota(jnp.int32, (tile, 1), 0) + j * tile
    gdim = uc_ref.shape[1] // len(POOL_WINDOWS)

    def window_sum(g):
        cols = slice(g * gdim, (g + 1) * gdim)
        wsum = jnp.dot(pb_ref[g], uc_ref[:, cols], preferred_element_type=F32)
        head = wsum[0:POOL_HALO] + jnp.dot(ph_ref[g], uh[:, cols], preferred_element_type=F32)
        return jnp.concatenate([head, wsum[POOL_HALO:]], axis=0)

    def delta(g, wsum):
        cols = slice(g * gdim, (g + 1) * gdim)
        cnt = jnp.minimum(t_glob + 1, POOL_WINDOWS[g]).astype(F32)
        return (wsum / cnt - uc_ref[:, cols].astype(F32)).astype(BF16)

    def pool_out(g, dlt):
        cols = slice(g * gdim, (g + 1) * gdim)
        yp = jnp.dot(dlt, wp_ref[g], preferred_element_type=F32) * ps_ref[:, cols]
        y_ref[:, d_attn + g * gdim:d_attn + (g + 1) * gdim] = yp.astype(BF16)

    def project_attn(c, n_chunks):
        n = o_ref.shape[1] // n_chunks
        cols = slice(c * n, (c + 1) * n)
        o_ref[:, cols] = h_ref[:, cols] + jnp.dot(
            y_ref[:, 0:d_attn], wo_ref[0:d_attn, cols], preferred_element_type=F32)

    n_groups = len(POOL_WINDOWS)
    wsums = [window_sum(g) for g in range(n_groups)]
    project_attn(0, n_groups)
    for g in range(n_groups):
        pool_out(g, delta(g, wsums[g]))
    for c in range(1, n_groups):
        project_attn(c, n_groups)
    d_mix = y_ref.shape[1]
    o_ref[...] += jnp.dot(y_ref[:, d_attn:d_mix], wo_ref[d_attn:d_mix, :],
                          preferred_element_type=F32)


def _pool_bands(tile):
    t = jnp.arange(tile)[:, None]
    c = jnp.arange(tile)[None, :]
    th = jnp.arange(POOL_HALO)[:, None]
    ch = jnp.arange(POOL_HALO)[None, :] - POOL_HALO
    inner = [((c <= t) & (c > t - w)) for w in POOL_WINDOWS]
    halo = [(ch > th - w) for w in POOL_WINDOWS]
    return jnp.stack(inner).astype(BF16), jnp.stack(halo).astype(BF16)


def _mix(h, z, bias, w_pool, pool_scale, w_out, *, batch, seq):
    t, d = h.shape
    d_attn = ATTN_HEADS * ATTN_HEAD_DIM
    d_pool = z.shape[1] - 3 * d_attn
    assert d_attn == d_pool and max(POOL_WINDOWS) - 1 <= POOL_HALO
    tile = TOKEN_TILE
    tps = seq // tile
    hpt = tile // POOL_HALO
    pband, phalo = _pool_bands(tile)

    def cur(col):
        return pl.BlockSpec((tile, d_attn), lambda b, j: (b * tps + j, col))

    def prev(col):
        return pl.BlockSpec((tile, d_attn), lambda b, j: (jnp.maximum(b * tps + j - 1, 0), col))

    halo_cols = d_pool // d_attn * 3
    in_specs = [
        pl.BlockSpec((tile, d), lambda b, j: (b * tps + j, 0)),
        cur(0),
        prev(1), cur(1),
        prev(2), cur(2),
        pl.BlockSpec((POOL_HALO, d_pool),
                     lambda b, j: (jnp.maximum((b * tps + j) * hpt - 1, 0), halo_cols)),
        cur(3),
        _const_spec(bias.shape),
        _const_spec(pband.shape),
        _const_spec(phalo.shape),
        _const_spec(w_pool.shape),
        _const_spec(pool_scale.shape),
        _const_spec(w_out.shape),
    ]
    scratch = [((tile, d_attn + d_pool), BF16)]
    windows = (2 * 2 * _nbytes((tile, d), F32) + 2 * 6 * _nbytes((tile, d_attn), BF16)
               + _nbytes(bias.shape, F32) + _nbytes(pband.shape, BF16)
               + _nbytes(w_pool.shape, BF16) + _nbytes(w_out.shape, BF16)
               + sum(_nbytes(s, dt) for s, dt in scratch))
    return pl.pallas_call(
        _mix_body,
        grid=(batch, tps),
        in_specs=in_specs,
        out_specs=pl.BlockSpec((tile, d), lambda b, j: (b * tps + j, 0)),
        out_shape=jax.ShapeDtypeStruct((t, d), F32),
        scratch_shapes=[pltpu.VMEM(s, dt) for s, dt in scratch],
        compiler_params=pltpu.CompilerParams(
            dimension_semantics=("parallel", "arbitrary"),
            vmem_limit_bytes=_vmem_limit(windows)),
        name="mix",
    )(h, z, z, z, z, z, z, z, bias, pband, phalo, w_pool, pool_scale, w_out)


def _bias_body(base_ref, o_ref):
    heads = base_ref.shape[0]
    width = base_ref.shape[2]
    qi = lax.broadcasted_iota(jnp.int32, (Q_BLOCK, KEY_BAND), 0)
    kb = lax.broadcasted_iota(jnp.int32, (Q_BLOCK, KEY_BAND), 1)
    q_chunk_start = qi & ~(CHUNK - 1)
    valid = (kb >= q_chunk_start) & (kb < q_chunk_start + LEFT + CHUNK)
    for h in range(heads):
        row = jnp.broadcast_to(base_ref[h], (Q_BLOCK, width))
        toep = pltpu.roll(row, 0, 1, stride=1, stride_axis=0)
        o_ref[h // 2, (h % 2) * Q_BLOCK:(h % 2 + 1) * Q_BLOCK, :] = jnp.where(
            valid, toep[:, 0:KEY_BAND], NEG_INF)


def _attn_bias(rel_table):
    heads = rel_table.shape[0]
    assert KEY_BAND - 1 - LEFT <= REL_CLIP
    far = rel_table[:, 2 * REL_CLIP:]
    base = jnp.concatenate([
        jnp.broadcast_to(far, (heads, LEFT - REL_CLIP + 1)),
        rel_table[:, 2 * REL_CLIP - 1:REL_CLIP + LEFT - KEY_BAND:-1],
        jnp.broadcast_to(far, (heads, Q_BLOCK)),
    ], axis=1) * LOG2_E
    width = KEY_BAND + Q_BLOCK
    assert base.shape == (heads, width)
    return pl.pallas_call(
        _bias_body,
        out_shape=jax.ShapeDtypeStruct((heads // 2, 2 * Q_BLOCK, KEY_BAND), F32),
        name="attn_bias",
    )(base.reshape(heads, 1, width))


def _cross_body(h_ref, wq_ref, k_ref, v_ref, wo_ref, o_ref, oc_ref):
    tile = h_ref.shape[0]
    half = tile // 2
    xn = [_rms(h_ref[r:r + half, :]).astype(BF16) for r in (0, half)]
    q = jnp.concatenate(
        [jnp.dot(x, wq_ref[...], preferred_element_type=F32).astype(BF16) for x in xn], axis=0)
    scale2 = CROSS_HEAD_DIM ** -0.5 * LOG2_E
    head_cols = [slice(hh * CROSS_HEAD_DIM, (hh + 1) * CROSS_HEAD_DIM)
                 for hh in range(CROSS_HEADS)]
    s_all = [lax.dot_general(q[:, c], k_ref[:, c], _NT, preferred_element_type=F32) * scale2
             for c in head_cols]
    e_all = [jnp.exp2(s - jnp.max(s, axis=-1, keepdims=True)).astype(BF16) for s in s_all]
    ones_cols = jnp.ones((k_ref.shape[0], CROSS_HEAD_DIM), BF16)
    for c, e in zip(head_cols, e_all):
        rhs = jnp.concatenate([v_ref[:, c], ones_cols], axis=1)
        ol = jnp.dot(e, rhs, preferred_element_type=F32)
        oc_ref[:, c] = (ol[:, 0:CROSS_HEAD_DIM] / ol[:, CROSS_HEAD_DIM:]).astype(BF16)
    o_ref[...] = h_ref[...] + jnp.dot(oc_ref[...], wo_ref[...], preferred_element_type=F32)


def _cross(h, w_cq, kv, w_co, *, seq, n_mem):
    t, d = h.shape
    dc = w_cq.shape[1]
    tps = seq // TOKEN_TILE
    windows = (2 * 2 * _nbytes((TOKEN_TILE, d), F32) + 2 * 2 * _nbytes((n_mem, dc), BF16)
               + 2 * _nbytes((d, dc), BF16) + _nbytes((TOKEN_TILE, dc), BF16))
    return pl.pallas_call(
        _cross_body,
        grid=(t // TOKEN_TILE,),
        in_specs=[
            pl.BlockSpec((TOKEN_TILE, d), lambda i: (i, 0)),
            _const_spec(w_cq.shape),
            pl.BlockSpec((n_mem, dc), lambda i: (i // tps, 0)),
            pl.BlockSpec((n_mem, dc), lambda i: (i // tps, 1)),
            _const_spec(w_co.shape),
        ],
        out_specs=pl.BlockSpec((TOKEN_TILE, d), lambda i: (i, 0)),
        out_shape=jax.ShapeDtypeStruct((t, d), F32),
        scratch_shapes=[pltpu.VMEM((TOKEN_TILE, dc), BF16)],
        compiler_params=pltpu.CompilerParams(
            dimension_semantics=("parallel",), vmem_limit_bytes=_vmem_limit(windows)),
        name="cross",
    )(h, w_cq, kv, kv, w_co)


def _fold(gain, w):
    return (gain[:, None] * w).astype(BF16)


def kernel(x, mem, ffn1_norm, ffn1_w_gate, ffn1_w_up, ffn1_w_down, mix_norm, w_in, rel_bias,
           w_pool, pool_scale, w_out, cross_norm, mem_norm, w_cq, w_ckv, w_co, ffn2_norm,
           ffn2_w_gate, ffn2_w_up, ffn2_w_down, final_norm):
    batch, seq, d = x.shape
    n_mem = mem.shape[1]
    depth = ffn1_norm.shape[0]
    d_attn = ATTN_HEADS * ATTN_HEAD_DIM
    assert seq % TOKEN_TILE == 0 and TOKEN_TILE % Q_BLOCK == 0 and TOKEN_TILE >= LEFT

    h = x.reshape(batch * seq, d)
    mem2 = mem.reshape(batch * n_mem, d)
    fin = final_norm.reshape(1, d)
    qscale = jnp.where(jnp.arange(w_in.shape[-1]) < d_attn,
                       ATTN_HEAD_DIM ** -0.5 * LOG2_E, 1.0).astype(F32)

    for l in range(depth):
        last = l == depth - 1
        h = _ffn(h, _fold(ffn1_norm[l], ffn1_w_gate[l]), _fold(ffn1_norm[l], ffn1_w_up[l]),
                 ffn1_w_down[l], fin, final_norm=False)
        z = _proj(h, mix_norm[l], w_in[l], qscale, n_split=PROJ_SPLIT, name="proj")
        h = _mix(h, z, _attn_bias(rel_bias[l]), w_pool[l].astype(BF16),
                 pool_scale[l].reshape(1, -1), w_out[l].astype(BF16), batch=batch, seq=seq)
        kv = _proj(mem2, mem_norm[l], w_ckv[l], jnp.ones((w_ckv.shape[-1],), F32),
                   n_split=1, name="memkv")
        h = _cross(h, _fold(cross_norm[l], w_cq[l]), kv, w_co[l].astype(BF16),
                   seq=seq, n_mem=n_mem)
        h = _ffn(h, _fold(ffn2_norm[l], ffn2_w_gate[l]), _fold(ffn2_norm[l], ffn2_w_up[l]),
                 ffn2_w_down[l], fin, final_norm=last)
    return h.reshape(batch, seq, d)
```

```python
import functools

import jax
import jax.numpy as jnp
from jax import lax
from jax.experimental import pallas as pl
from jax.experimental.pallas import tpu as pltpu

F32 = jnp.float32
BF16 = jnp.bfloat16

EPS = 1e-6
NEG_INF = -1e30
FFN_RES_WEIGHT = 0.5
LOG2_E = 1.4426950408889634

CHUNK = 64
LEFT_CHUNKS = 8
REL_CLIP = 128
ATTN_HEADS = 16
ATTN_HEAD_DIM = 64
POOL_WINDOWS = (2, 4, 8, 16)
CROSS_HEADS = 4
CROSS_HEAD_DIM = 128

V7X_LANES = 128
V7X_MXU_WIDTH = 256
V7X_VMEM_BYTES = 64 * 1024 * 1024
V7X_VMEM_RESERVE = 2 * 1024 * 1024

TOKEN_TILE = 512
FFN_TOKEN_TILE = 1024
FF_TILE = 512
FF_SUBCHUNKS = 2
PROJ_SPLIT = 4
Q_BLOCK = 2 * CHUNK
LEFT = LEFT_CHUNKS * CHUNK
KEY_BAND = LEFT + Q_BLOCK
SCORE_LEAD = 3
POOL_HALO = 16
TEMP_VMEM_BYTES = 11 * 1024 * 1024

_NT = (((1,), (1,)), ((), ()))


def _nbytes(shape, dtype):
    n = 1
    for s in shape:
        n *= s
    return n * jnp.dtype(dtype).itemsize


def _vmem_limit(window_bytes):
    return min(window_bytes + TEMP_VMEM_BYTES, V7X_VMEM_BYTES - V7X_VMEM_RESERVE)


def _rms(x):
    ms = jnp.mean(x * x, axis=-1, keepdims=True)
    return x * lax.rsqrt(ms + EPS)


def _const_spec(shape):
    nd = len(shape)
    return pl.BlockSpec(shape, lambda *_: (0,) * nd, pipeline_mode=pl.Buffered(1))


def _ffn_body(x_ref, wg_ref, wu_ref, wd_ref, fin_ref, o_ref, xn_ref, *, final_norm):
    k = pl.program_id(1)

    @pl.when(k == 0)
    def _():
        x = x_ref[...]
        xn_ref[...] = _rms(x).astype(BF16)
        o_ref[...] = x

    sub = wg_ref.shape[1] // FF_SUBCHUNKS

    def hidden():
        xn = xn_ref[...]
        gu = []
        for c in range(FF_SUBCHUNKS):
            cols = slice(c * sub, (c + 1) * sub)
            gu.append((jnp.dot(xn, wg_ref[:, cols], preferred_element_type=F32),
                       jnp.dot(xn, wu_ref[:, cols], preferred_element_type=F32)))
        return [((g * jax.nn.sigmoid(g)) * (u * FFN_RES_WEIGHT)).astype(BF16) for g, u in gu]

    def down(acts, rows):
        acc = None
        for c, a in enumerate(acts):
            wd = wd_ref[c * sub:(c + 1) * sub, :].astype(BF16)
            d = jnp.dot(a[rows, :], wd, preferred_element_type=F32)
            acc = d if acc is None else acc + d
        return acc

    def accumulate():
        o_ref[...] += down(hidden(), slice(None))

    if not final_norm:
        accumulate()
    else:
        last = pl.num_programs(1) - 1
        pl.when(k < last)(accumulate)

        @pl.when(k == last)
        def _():
            acts = hidden()
            half = o_ref.shape[0] // 2
            for r in (0, half):
                rows = slice(r, r + half)
                o_ref[rows, :] = _rms(o_ref[rows, :] + down(acts, rows)) * fin_ref[...]


def _ffn(h, wg, wu, wd, fin, *, final_norm):
    t, d = h.shape
    dff = wg.shape[1]
    tm = FFN_TOKEN_TILE
    assert FF_TILE // FF_SUBCHUNKS >= V7X_MXU_WIDTH
    windows = (2 * 2 * _nbytes((tm, d), F32) + 2 * 2 * _nbytes((d, FF_TILE), BF16)
               + 2 * _nbytes((FF_TILE, d), wd.dtype) + _nbytes((tm, d), BF16))
    return pl.pallas_call(
        functools.partial(_ffn_body, final_norm=final_norm),
        grid=(t // tm, dff // FF_TILE),
        in_specs=[
            pl.BlockSpec((tm, d), lambda i, k: (i, 0)),
            pl.BlockSpec((d, FF_TILE), lambda i, k: (0, k)),
            pl.BlockSpec((d, FF_TILE), lambda i, k: (0, k)),
            pl.BlockSpec((FF_TILE, d), lambda i, k: (k, 0)),
            pl.BlockSpec((1, d), lambda i, k: (0, 0)),
        ],
        out_specs=pl.BlockSpec((tm, d), lambda i, k: (i, 0)),
        out_shape=jax.ShapeDtypeStruct((t, d), F32),
        scratch_shapes=[pltpu.VMEM((tm, d), BF16)],
        compiler_params=pltpu.CompilerParams(
            dimension_semantics=("parallel", "arbitrary"),
            vmem_limit_bytes=_vmem_limit(windows)),
        name="ffn_final" if final_norm else "ffn",
    )(h, wg, wu, wd, fin)


def _proj_body(h_ref, g_ref, w_ref, cs_ref, z_ref, *, n_split):
    half = h_ref.shape[0] // 2
    xn = [(_rms(h_ref[r:r + half, :]) * g_ref[...]).astype(BF16) for r in (0, half)]
    n = w_ref.shape[1] // n_split
    w0 = w_ref[:, 0:n].astype(BF16)
    for r, x in zip((0, half), xn):
        z = jnp.dot(x, w0, preferred_element_type=F32) * cs_ref[:, 0:n]
        z_ref[r:r + half, 0:n] = z.astype(z_ref.dtype)
    xn = jnp.concatenate(xn, axis=0)
    for c in range(1, n_split):
        cols = slice(c * n, (c + 1) * n)
        z = jnp.dot(xn, w_ref[:, cols].astype(BF16), preferred_element_type=F32) * cs_ref[:, cols]
        z_ref[:, cols] = z.astype(z_ref.dtype)


def _proj(h, gain, w, col_scale, *, n_split, name):
    t, d = h.shape
    n = w.shape[1]
    windows = (2 * _nbytes((TOKEN_TILE, d), F32) + 2 * _nbytes((TOKEN_TILE, n), BF16)
               + _nbytes((d, n), F32))
    return pl.pallas_call(
        functools.partial(_proj_body, n_split=n_split),
        grid=(t // TOKEN_TILE,),
        in_specs=[
            pl.BlockSpec((TOKEN_TILE, d), lambda i: (i, 0)),
            _const_spec((1, d)),
            _const_spec((d, n)),
            _const_spec((1, n)),
        ],
        out_specs=pl.BlockSpec((TOKEN_TILE, n), lambda i: (i, 0)),
        out_shape=jax.ShapeDtypeStruct((t, n), BF16),
        compiler_params=pltpu.CompilerParams(
            dimension_semantics=("parallel",), vmem_limit_bytes=_vmem_limit(windows)),
        name=name,
    )(h, gain.reshape(1, d), w, col_scale.reshape(1, n))


def _mix_body(h_ref, q_ref, kp_ref, kc_ref, vp_ref, vc_ref, uh_ref, uc_ref, bias_ref,
              pb_ref, ph_ref, wp_ref, ps_ref, wo_ref, o_ref, y_ref):
    j = pl.program_id(1)
    tile = q_ref.shape[0]
    d_attn = q_ref.shape[1]
    n_pairs = d_attn // V7X_LANES

    lane = lax.broadcasted_iota(jnp.int32, (Q_BLOCK, V7X_LANES), 1)
    low = lane < ATTN_HEAD_DIM
    ones_cols = jnp.ones((KEY_BAND, V7X_LANES), BF16)
    kbi = lax.broadcasted_iota(jnp.int32, (1, KEY_BAND), 1)

    def band(prev_ref, cur_ref, i, cols):
        b0 = tile - LEFT + i * Q_BLOCK
        parts = [prev_ref[b0:tile, cols]] if b0 < tile else []
        parts.append(cur_ref[max(b0 - tile, 0):b0 + KEY_BAND - tile, cols])
        return jnp.concatenate(parts, axis=0)

    def biased_scores(i, p):
        cols = slice(p * V7X_LANES, (p + 1) * V7X_LANES)
        qb = q_ref[i * Q_BLOCK:(i + 1) * Q_BLOCK, cols]
        zero = jnp.zeros_like(qb)
        qbd = jnp.concatenate([jnp.where(low, qb, zero), jnp.where(low, zero, qb)], axis=0)
        s = lax.dot_general(qbd, band(kp_ref, kc_ref, i, cols), _NT, preferred_element_type=F32)
        kpos = j * tile + (i * Q_BLOCK - LEFT) + kbi
        return s + (bias_ref[p] + jnp.where(kpos >= 0, 0.0, NEG_INF).astype(F32))

    units = [(i, p) for i in range(tile // Q_BLOCK) for p in range(n_pairs)]
    s_all = [biased_scores(i, p) for i, p in units[:SCORE_LEAD]]
    for n, (i, p) in enumerate(units):
        s = s_all[n]
        s_all[n] = None
        m = jnp.max(s, axis=-1, keepdims=True)
        e = jnp.exp2(s - m).astype(BF16)
        if n + SCORE_LEAD < len(units):
            s_all.append(biased_scores(*units[n + SCORE_LEAD]))
        cols = slice(p * V7X_LANES, (p + 1) * V7X_LANES)
        rhs = jnp.concatenate([band(vp_ref, vc_ref, i, cols), ones_cols], axis=1)
        ol = jnp.dot(e, rhs, preferred_element_type=F32)
        o = ol[:, 0:V7X_LANES] / ol[:, V7X_LANES:2 * V7X_LANES]
        op = jnp.where(low, o[0:Q_BLOCK], o[Q_BLOCK:2 * Q_BLOCK])
        y_ref[i * Q_BLOCK:(i + 1) * Q_BLOCK, cols] = op.astype(BF16)

    uh = uh_ref[...]
    uh = jnp.where(j == 0, jnp.zeros_like(uh), uh)
    t_glob = lax.broadcasted_iota(jnp.int32, (tile, 1), 0) + j * tile
    gdim = uc_ref.shape[1] // len(POOL_WINDOWS)

    def window_sum(g):
        cols = slice(g * gdim, (g + 1) * gdim)
        wsum = jnp.dot(pb_ref[g], uc_ref[:, cols], preferred_element_type=F32)
        head = wsum[0:POOL_HALO] + jnp.dot(ph_ref[g], uh[:, cols], preferred_element_type=F32)
        return jnp.concatenate([head, wsum[POOL_HALO:]], axis=0)

    def delta(g, wsum):
        cols = slice(g * gdim, (g + 1) * gdim)
        cnt = jnp.minimum(t_glob + 1, POOL_WINDOWS[g]).astype(F32)
        return (wsum / cnt - uc_ref[:, cols].astype(F32)).astype(BF16)

    def pool_out(g, dlt):
        cols = slice(g * gdim, (g + 1) * gdim)
        yp = jnp.dot(dlt, wp_ref[g], preferred_element_type=F32) * ps_ref[:, cols]
        y_ref[:, d_attn + g * gdim:d_attn + (g + 1) * gdim] = yp.astype(BF16)

    def project_attn(c, n_chunks):
        n = o_ref.shape[1] // n_chunks
        cols = slice(c * n, (c + 1) * n)
        o_ref[:, cols] = h_ref[:, cols] + jnp.dot(
            y_ref[:, 0:d_attn], wo_ref[0:d_attn, cols], preferred_element_type=F32)

    n_groups = len(POOL_WINDOWS)
    wsums = [window_sum(g) for g in range(n_groups)]
    project_attn(0, n_groups)
    for g in range(n_groups):
        pool_out(g, delta(g, wsums[g]))
    for c in range(1, n_groups):
        project_attn(c, n_groups)
    d_mix = y_ref.shape[1]
    o_ref[...] += jnp.dot(y_ref[:, d_attn:d_mix], wo_ref[d_attn:d_mix, :],
                          preferred_element_type=F32)


def _pool_bands(tile):
    t = jnp.arange(tile)[:, None]
    c = jnp.arange(tile)[None, :]
    th = jnp.arange(POOL_HALO)[:, None]
    ch = jnp.arange(POOL_HALO)[None, :] - POOL_HALO
    inner = [((c <= t) & (c > t - w)) for w in POOL_WINDOWS]
    halo = [(ch > th - w) for w in POOL_WINDOWS]
    return jnp.stack(inner).astype(BF16), jnp.stack(halo).astype(BF16)


def _mix(h, z, bias, w_pool, pool_scale, w_out, *, batch, seq):
    t, d = h.shape
    d_attn = ATTN_HEADS * ATTN_HEAD_DIM
    d_pool = z.shape[1] - 3 * d_attn
    assert d_attn == d_pool and max(POOL_WINDOWS) - 1 <= POOL_HALO
    tile = TOKEN_TILE
    tps = seq // tile
    hpt = tile // POOL_HALO
    pband, phalo = _pool_bands(tile)

    def cur(col):
        return pl.BlockSpec((tile, d_attn), lambda b, j: (b * tps + j, col))

    def prev(col):
        return pl.BlockSpec((tile, d_attn), lambda b, j: (jnp.maximum(b * tps + j - 1, 0), col))

    halo_cols = d_pool // d_attn * 3
    in_specs = [
        pl.BlockSpec((tile, d), lambda b, j: (b * tps + j, 0)),
        cur(0),
        prev(1), cur(1),
        prev(2), cur(2),
        pl.BlockSpec((POOL_HALO, d_pool),
                     lambda b, j: (jnp.maximum((b * tps + j) * hpt - 1, 0), halo_cols)),
        cur(3),
        _const_spec(bias.shape),
        _const_spec(pband.shape),
        _const_spec(phalo.shape),
        _const_spec(w_pool.shape),
        _const_spec(pool_scale.shape),
        _const_spec(w_out.shape),
    ]
    scratch = [((tile, d_attn + d_pool), BF16)]
    windows = (2 * 2 * _nbytes((tile, d), F32) + 2 * 6 * _nbytes((tile, d_attn), BF16)
               + _nbytes(bias.shape, F32) + _nbytes(pband.shape, BF16)
               + _nbytes(w_pool.shape, BF16) + _nbytes(w_out.shape, BF16)
               + sum(_nbytes(s, dt) for s, dt in scratch))
    return pl.pallas_call(
        _mix_body,
        grid=(batch, tps),
        in_specs=in_specs,
        out_specs=pl.BlockSpec((tile, d), lambda b, j: (b * tps + j, 0)),
        out_shape=jax.ShapeDtypeStruct((t, d), F32),
        scratch_shapes=[pltpu.VMEM(s, dt) for s, dt in scratch],
        compiler_params=pltpu.CompilerParams(
            dimension_semantics=("parallel", "arbitrary"),
            vmem_limit_bytes=_vmem_limit(windows)),
        name="mix",
    )(h, z, z, z, z, z, z, z, bias, pband, phalo, w_pool, pool_scale, w_out)


def _bias_body(base_ref, o_ref):
    heads = base_ref.shape[0]
    width = base_ref.shape[2]
    qi = lax.broadcasted_iota(jnp.int32, (Q_BLOCK, KEY_BAND), 0)
    kb = lax.broadcasted_iota(jnp.int32, (Q_BLOCK, KEY_BAND), 1)
    q_chunk_start = qi & ~(CHUNK - 1)
    valid = (kb >= q_chunk_start) & (kb < q_chunk_start + LEFT + CHUNK)
    for h in range(heads):
        row = jnp.broadcast_to(base_ref[h], (Q_BLOCK, width))
        toep = pltpu.roll(row, 0, 1, stride=1, stride_axis=0)
        o_ref[h // 2, (h % 2) * Q_BLOCK:(h % 2 + 1) * Q_BLOCK, :] = jnp.where(
            valid, toep[:, 0:KEY_BAND], NEG_INF)


def _attn_bias(rel_table):
    heads = rel_table.shape[0]
    assert KEY_BAND - 1 - LEFT <= REL_CLIP
    far = rel_table[:, 2 * REL_CLIP:]
    base = jnp.concatenate([
        jnp.broadcast_to(far, (heads, LEFT - REL_CLIP + 1)),
        rel_table[:, 2 * REL_CLIP - 1:REL_CLIP + LEFT - KEY_BAND:-1],
        jnp.broadcast_to(far, (heads, Q_BLOCK)),
    ], axis=1) * LOG2_E
    width = KEY_BAND + Q_BLOCK
    assert base.shape == (heads, width)
    return pl.pallas_call(
        _bias_body,
        out_shape=jax.ShapeDtypeStruct((heads // 2, 2 * Q_BLOCK, KEY_BAND), F32),
        name="attn_bias",
    )(base.reshape(heads, 1, width))


def _cross_body(h_ref, mem_ref, gm_ref, wkv_ref, wq_ref, wo_ref, o_ref, oc_ref, kv_ref, *, tps):
    @pl.when(pl.program_id(0) % tps == 0)
    def _():
        xm = (_rms(mem_ref[...]) * gm_ref[...]).astype(BF16)
        kv_ref[...] = jnp.dot(xm, wkv_ref[...].astype(BF16),
                              preferred_element_type=F32).astype(BF16)

    dc = wq_ref.shape[1]
    k_ref = kv_ref.at[:, 0:dc]
    v_ref = kv_ref.at[:, dc:2 * dc]
    tile = h_ref.shape[0]
    half = tile // 2
    xn = [_rms(h_ref[r:r + half, :]).astype(BF16) for r in (0, half)]
    q = jnp.concatenate(
        [jnp.dot(x, wq_ref[...], preferred_element_type=F32).astype(BF16) for x in xn], axis=0)
    scale2 = CROSS_HEAD_DIM ** -0.5 * LOG2_E
    head_cols = [slice(hh * CROSS_HEAD_DIM, (hh + 1) * CROSS_HEAD_DIM)
                 for hh in range(CROSS_HEADS)]
    s_all = [lax.dot_general(q[:, c], k_ref[:, c], _NT, preferred_element_type=F32) * scale2
             for c in head_cols]
    e_all = [jnp.exp2(s - jnp.max(s, axis=-1, keepdims=True)).astype(BF16) for s in s_all]
    ones_cols = jnp.ones((k_ref.shape[0], CROSS_HEAD_DIM), BF16)
    for c, e in zip(head_cols, e_all):
        rhs = jnp.concatenate([v_ref[:, c], ones_cols], axis=1)
        ol = jnp.dot(e, rhs, preferred_element_type=F32)
        oc_ref[:, c] = (ol[:, 0:CROSS_HEAD_DIM] / ol[:, CROSS_HEAD_DIM:]).astype(BF16)
    o_ref[...] = h_ref[...] + jnp.dot(oc_ref[...], wo_ref[...], preferred_element_type=F32)


def _cross(h, mem2, mem_gain, w_ckv, w_cq, w_co, *, seq, n_mem):
    t, d = h.shape
    dc = w_cq.shape[1]
    tps = seq // TOKEN_TILE
    windows = (2 * 2 * _nbytes((TOKEN_TILE, d), F32) + 2 * _nbytes((n_mem, d), F32)
               + _nbytes(w_ckv.shape, F32) + 2 * _nbytes((d, dc), BF16)
               + _nbytes((TOKEN_TILE, dc), BF16) + _nbytes((n_mem, 2 * dc), BF16))
    return pl.pallas_call(
        functools.partial(_cross_body, tps=tps),
        grid=(t // TOKEN_TILE,),
        in_specs=[
            pl.BlockSpec((TOKEN_TILE, d), lambda i: (i, 0)),
            pl.BlockSpec((n_mem, d), lambda i: (i // tps, 0)),
            _const_spec((1, d)),
            _const_spec(w_ckv.shape),
            _const_spec(w_cq.shape),
            _const_spec(w_co.shape),
        ],
        out_specs=pl.BlockSpec((TOKEN_TILE, d), lambda i: (i, 0)),
        out_shape=jax.ShapeDtypeStruct((t, d), F32),
        scratch_shapes=[pltpu.VMEM((TOKEN_TILE, dc), BF16), pltpu.VMEM((n_mem, 2 * dc), BF16)],
        compiler_params=pltpu.CompilerParams(
            dimension_semantics=("arbitrary",), vmem_limit_bytes=_vmem_limit(windows)),
        name="cross",
    )(h, mem2, mem_gain.reshape(1, d), w_ckv, w_cq, w_co)


def _fold(gain, w):
    return (gain[:, None] * w).astype(BF16)


def kernel(x, mem, ffn1_norm, ffn1_w_gate, ffn1_w_up, ffn1_w_down, mix_norm, w_in, rel_bias,
           w_pool, pool_scale, w_out, cross_norm, mem_norm, w_cq, w_ckv, w_co, ffn2_norm,
           ffn2_w_gate, ffn2_w_up, ffn2_w_down, final_norm):
    batch, seq, d = x.shape
    n_mem = mem.shape[1]
    depth = ffn1_norm.shape[0]
    d_attn = ATTN_HEADS * ATTN_HEAD_DIM
    assert seq % TOKEN_TILE == 0 and TOKEN_TILE % Q_BLOCK == 0 and TOKEN_TILE >= LEFT

    h = x.reshape(batch * seq, d)
    mem2 = mem.reshape(batch * n_mem, d)
    fin = final_norm.reshape(1, d)
    qscale = jnp.where(jnp.arange(w_in.shape[-1]) < d_attn,
                       ATTN_HEAD_DIM ** -0.5 * LOG2_E, 1.0).astype(F32)

    for l in range(depth):
        last = l == depth - 1
        h = _ffn(h, _fold(ffn1_norm[l], ffn1_w_gate[l]), _fold(ffn1_norm[l], ffn1_w_up[l]),
                 ffn1_w_down[l], fin, final_norm=False)
        z = _proj(h, mix_norm[l], w_in[l], qscale, n_split=PROJ_SPLIT, name="proj")
        h = _mix(h, z, _attn_bias(rel_bias[l]), w_pool[l].astype(BF16),
                 pool_scale[l].reshape(1, -1), w_out[l].astype(BF16), batch=batch, seq=seq)
        h = _cross(h, mem2, mem_norm[l], w_ckv[l], _fold(cross_norm[l], w_cq[l]),
                   w_co[l].astype(BF16),
                   seq=seq, n_mem=n_mem)
        h = _ffn(h, _fold(ffn2_norm[l], ffn2_w_gate[l]), _fold(ffn2_norm[l], ffn2_w_up[l]),
                 ffn2_w_down[l], fin, final_norm=last)
    return h.reshape(batch, seq, d)
```
